```python
import jax, jax.numpy as jnp
from jax import lax
import numpy as np


D_MODEL = 1024
BATCH = 4
SEQ = 4096
DEPTH = 4

N_MIXERS = 3
N_A = (DEPTH + 2) // 3
N_B = (DEPTH + 1) // 3
N_C = DEPTH // 3
EPS = 1e-6

GM_CHUNK = 128
GM_GROUPS = 8
GM_INNER = 2 * D_MODEL
GM_GROUP_DIM = GM_INNER // GM_GROUPS

MLA_HEADS = 8
MLA_Q_RANK = D_MODEL // 4
MLA_KV_RANK = D_MODEL // 8
MLA_NOPE = 128
MLA_ROPE = 64
MLA_V = 128
ROPE_THETA = 10000.0
ATTN_BLOCK = 128

HG_HEADS = 8
HG_DK = 128
HG_DV = D_MODEL // HG_HEADS
HG_CHUNK = 64

FF_DIM = 2816
CONV_W = 3

kernel_name = 'hybrid_gmlp_mla_hgrn2_convffn_adaln'


def rmsnorm(x, g):
    xf = x.astype(jnp.float32)
    y = xf * lax.rsqrt(jnp.mean(xf * xf, axis=-1, keepdims=True) + EPS)
    return (y * g.astype(jnp.float32)).astype(x.dtype)


def layernorm(x, g, b):
    xf = x.astype(jnp.float32)
    mu = jnp.mean(xf, axis=-1, keepdims=True)
    var = jnp.mean(jnp.square(xf - mu), axis=-1, keepdims=True)
    y = (xf - mu) * lax.rsqrt(var + EPS) * g.astype(jnp.float32) + b.astype(jnp.float32)
    return y.astype(x.dtype)


def modulate(h, shift, scale):
    return h * (1.0 + scale[:, None, :]) + shift[:, None, :]


def rope_tables(positions):
    inv_freq = ROPE_THETA ** (-jnp.arange(0, MLA_ROPE, 2, dtype=jnp.float32) / MLA_ROPE)
    ang = positions.astype(jnp.float32)[..., None] * inv_freq
    return jnp.cos(ang)[:, :, None, :], jnp.sin(ang)[:, :, None, :]


def apply_rope(x, cos, sin):
    x1, x2 = jnp.split(x, 2, axis=-1)
    cos = cos.astype(x.dtype)
    sin = sin.astype(x.dtype)
    return jnp.concatenate([x1 * cos - x2 * sin, x2 * cos + x1 * sin], axis=-1)


def gmlp_mixer(h, w_in, ln_g, ln_b, w_s, b_s, w_out):
    B, S, _ = h.shape
    z = jax.nn.gelu(h @ w_in)
    u, v = jnp.split(z, 2, axis=-1)
    v = layernorm(v, ln_g, ln_b)
    n = S // GM_CHUNK
    v = v.reshape(B, n, GM_CHUNK, GM_GROUPS, GM_GROUP_DIM)
    mask = jnp.tril(jnp.ones((GM_CHUNK, GM_CHUNK), dtype=bool))
    ws = jnp.where(mask[None], w_s, jnp.zeros_like(w_s))
    sv = jnp.einsum('gts,bnsgd->bntgd', ws, v) + b_s.T[None, None, :, :, None]
    y = u * sv.reshape(B, S, GM_INNER)
    return y @ w_out


def causal_block_attention(q, k, v, scale):
    B, S, H, Dk = q.shape
    nb = S // ATTN_BLOCK
    qb = q.reshape(B, nb, ATTN_BLOCK, H, Dk).transpose(1, 0, 2, 3, 4)
    kpos = jnp.arange(S)

    def one_block(args):
        qi, bi = args
        s = jnp.einsum('bqhd,bkhd->bhqk', qi, k).astype(jnp.float32) * scale
        qpos = bi * ATTN_BLOCK + jnp.arange(ATTN_BLOCK)
        s = jnp.where(kpos[None, :] <= qpos[:, None], s, -jnp.inf)
        p = jax.nn.softmax(s, axis=-1).astype(v.dtype)
        return jnp.einsum('bhqk,bkhd->bqhd', p, v)

    o = lax.map(one_block, (qb, jnp.arange(nb)))
    return o.transpose(1, 0, 2, 3, 4).reshape(B, S, H, v.shape[-1])


def mla_mixer(h, cos, sin, w_a, q_norm_g, kv_norm_g, w_qb, w_kvb, w_o):
    B, S, _ = h.shape
    a = h @ w_a
    cq = a[..., :MLA_Q_RANK]
    ckv = a[..., MLA_Q_RANK:MLA_Q_RANK + MLA_KV_RANK]
    k_rope = a[..., MLA_Q_RANK + MLA_KV_RANK:]
    q = (rmsnorm(cq, q_norm_g) @ w_qb).reshape(B, S, MLA_HEADS, MLA_NOPE + MLA_ROPE)
    q_nope, q_rope = q[..., :MLA_NOPE], q[..., MLA_NOPE:]
    kv = (rmsnorm(ckv, kv_norm_g) @ w_kvb).reshape(B, S, MLA_HEADS, MLA_NOPE + MLA_V)
    k_nope, v = kv[..., :MLA_NOPE], kv[..., MLA_NOPE:]
    q_rope = apply_rope(q_rope, cos, sin)
    k_rope = apply_rope(k_rope[:, :, None, :], cos, sin)
    k_rope = jnp.broadcast_to(k_rope, (B, S, MLA_HEADS, MLA_ROPE))
    q = jnp.concatenate([q_nope, q_rope], axis=-1)
    k = jnp.concatenate([k_nope, k_rope], axis=-1)
    o = causal_block_attention(q, k, v, (MLA_NOPE + MLA_ROPE) ** -0.5)
    return o.reshape(B, S, MLA_HEADS * MLA_V) @ w_o


def hgrn_lower_bounds(lb_param):
    p = jax.nn.softmax(lb_param.astype(jnp.float32), axis=0)
    return jnp.cumsum(p, axis=0) - p[0:1]


def chunked_gated_recurrence(q, k, v, log_f):
    B, S, H, DK = q.shape
    DV = v.shape[-1]
    C = HG_CHUNK
    n = S // C

    def to_chunks(t):
        return t.astype(jnp.float32).reshape(B, n, C, H, t.shape[-1]).transpose(1, 0, 3, 2, 4)

    qc, kc, vc, gc = to_chunks(q), to_chunks(k), to_chunks(v), to_chunks(log_f)
    causal = jnp.tril(jnp.ones((C, C), dtype=bool))[:, :, None]

    def step(state, inp):
        qi, ki, vi, gi = inp
        bcum = jnp.cumsum(gi, axis=2)
        rel = bcum[:, :, :, None, :] - bcum[:, :, None, :, :]
        decay = jnp.exp(jnp.where(causal, rel, -jnp.inf))
        attn = jnp.einsum('bhtk,bhtsk,bhsk->bhts', qi, decay, ki)
        o = jnp.einsum('bhts,bhsv->bhtv', attn, vi) + jnp.einsum('bhtk,bhkv->bhtv', qi * jnp.exp(bcum), state)
        blast = bcum[:, :, -1:, :]
        new_state = jnp.exp(blast[:, :, 0, :])[..., None] * state + jnp.einsum('bhsk,bhsv->bhkv', ki * jnp.exp(blast - bcum), vi)
        return new_state, o

    state0 = jnp.zeros((B, H, DK, DV), jnp.float32)
    _, o = lax.scan(step, state0, (qc, kc, vc, gc))
    return o.transpose(1, 0, 3, 2, 4).reshape(B, S, H, DV).astype(v.dtype)


def hgrn2_mixer(h, lb, w_in, norm_g, w_o):
    B, S, _ = h.shape
    nk = HG_HEADS * HG_DK
    nv = HG_HEADS * HG_DV
    proj = h @ w_in
    q = proj[..., :nk]
    fl = proj[..., nk:2 * nk]
    i = proj[..., 2 * nk:2 * nk + nv]
    g = proj[..., 2 * nk + nv:]
    f = lb + (1.0 - lb) * jax.nn.sigmoid(fl.astype(jnp.float32))
    log_f = jnp.log(f)
    k = 1.0 - f
    o = chunked_gated_recurrence(
        q.reshape(B, S, HG_HEADS, HG_DK), k.reshape(B, S, HG_HEADS, HG_DK),
        i.reshape(B, S, HG_HEADS, HG_DV), log_f.reshape(B, S, HG_HEADS, HG_DK))
    o = rmsnorm(o, norm_g).reshape(B, S, nv) * jax.nn.silu(g)
    return o @ w_o


def conv_ffn(h, w_up, conv_w, conv_b, w_down):
    S = h.shape[1]
    a = h @ w_up
    pad = jnp.pad(a, ((0, 0), (CONV_W - 1, 0), (0, 0)))
    y = conv_b
    for j in range(CONV_W):
        y = y + conv_w[j] * pad[:, j:j + S, :]
    gate, val = jnp.split(y, 2, axis=-1)
    return (jax.nn.silu(gate) * val) @ w_down


def setup_inputs(seed: int = 0) -> dict:
    key = jax.random.key(seed)
    ks = jax.random.split(key, 32)
    f32 = jnp.float32

    def nrm(k, shape, scale):
        return jax.random.normal(k, shape, f32) * scale

    D = D_MODEL
    x = nrm(ks[0], (BATCH, SEQ, D), 1.0)
    c = nrm(ks[1], (BATCH, D), 1.0)
    offset = jax.random.randint(ks[2], (BATCH, 1), 0, 1024, dtype=jnp.int32)
    positions = offset + jnp.arange(SEQ, dtype=jnp.int32)[None, :]
    ada_w = nrm(ks[3], (DEPTH, D, 6 * D), 0.2 * D ** -0.5)
    ada_b = nrm(ks[4], (DEPTH, 6 * D), 0.02)
    mix_norm_g = 1.0 + nrm(ks[5], (DEPTH, D), 0.05)
    ffn_norm_g = 1.0 + nrm(ks[6], (DEPTH, D), 0.05)
    gm_w_in = nrm(ks[7], (N_A, D, 2 * GM_INNER), D ** -0.5)
    gm_ln_g = 1.0 + nrm(ks[8], (N_A, GM_INNER), 0.05)
    gm_ln_b = nrm(ks[9], (N_A, GM_INNER), 0.02)
    gm_w_s = nrm(ks[10], (N_A, GM_GROUPS, GM_CHUNK, GM_CHUNK), GM_CHUNK ** -0.5)
    gm_b_s = 1.0 + nrm(ks[11], (N_A, GM_GROUPS, GM_CHUNK), 0.1)
    gm_w_out = nrm(ks[12], (N_A, GM_INNER, D), GM_INNER ** -0.5)
    mla_w_a = nrm(ks[13], (N_B, D, MLA_Q_RANK + MLA_KV_RANK + MLA_ROPE), D ** -0.5)
    mla_q_norm_g = 1.0 + nrm(ks[14], (N_B, MLA_Q_RANK), 0.05)
    mla_kv_norm_g = 1.0 + nrm(ks[15], (N_B, MLA_KV_RANK), 0.05)
    mla_w_qb = nrm(ks[16], (N_B, MLA_Q_RANK, MLA_HEADS * (MLA_NOPE + MLA_ROPE)), MLA_Q_RANK ** -0.5)
    mla_w_kvb = nrm(ks[17], (N_B, MLA_KV_RANK, MLA_HEADS * (MLA_NOPE + MLA_V)), MLA_KV_RANK ** -0.5)
    mla_w_o = nrm(ks[18], (N_B, MLA_HEADS * MLA_V, D), (MLA_HEADS * MLA_V) ** -0.5)
    hg_lb = nrm(ks[19], (DEPTH, HG_HEADS * HG_DK), 0.5)
    hg_w_in = nrm(ks[20], (N_C, D, 2 * HG_HEADS * HG_DK + 2 * HG_HEADS * HG_DV), D ** -0.5)
    hg_norm_g = 1.0 + nrm(ks[21], (N_C, HG_DV), 0.05)
    hg_w_o = nrm(ks[22], (N_C, HG_HEADS * HG_DV, D), (HG_HEADS * HG_DV) ** -0.5)
    ff_w_up = nrm(ks[23], (DEPTH, D, 2 * FF_DIM), D ** -0.5)
    ff_conv_w = nrm(ks[24], (DEPTH, CONV_W, 2 * FF_DIM), CONV_W ** -0.5)
    ff_conv_b = nrm(ks[25], (DEPTH, 2 * FF_DIM), 0.02)
    ff_w_down = nrm(ks[26], (DEPTH, FF_DIM, D), FF_DIM ** -0.5)
    final_g = 1.0 + nrm(ks[27], (D,), 0.05)
    return {'x': x, 'c': c, 'positions': positions, 'ada_w': ada_w, 'ada_b': ada_b,
            'mix_norm_g': mix_norm_g, 'ffn_norm_g': ffn_norm_g,
            'gm_w_in': gm_w_in, 'gm_ln_g': gm_ln_g, 'gm_ln_b': gm_ln_b, 'gm_w_s': gm_w_s,
            'gm_b_s': gm_b_s, 'gm_w_out': gm_w_out,
            'mla_w_a': mla_w_a, 'mla_q_norm_g': mla_q_norm_g, 'mla_kv_norm_g': mla_kv_norm_g,
            'mla_w_qb': mla_w_qb, 'mla_w_kvb': mla_w_kvb, 'mla_w_o': mla_w_o,
            'hg_lb': hg_lb, 'hg_w_in': hg_w_in, 'hg_norm_g': hg_norm_g, 'hg_w_o': hg_w_o,
            'ff_w_up': ff_w_up, 'ff_conv_w': ff_conv_w, 'ff_conv_b': ff_conv_b, 'ff_w_down': ff_w_down,
            'final_g': final_g}


def reference(x, c, positions, ada_w, ada_b, mix_norm_g, ffn_norm_g,
              gm_w_in, gm_ln_g, gm_ln_b, gm_w_s, gm_b_s, gm_w_out,
              mla_w_a, mla_q_norm_g, mla_kv_norm_g, mla_w_qb, mla_w_kvb, mla_w_o,
              hg_lb, hg_w_in, hg_norm_g, hg_w_o,
              ff_w_up, ff_conv_w, ff_conv_b, ff_w_down, final_g):
    cos, sin = rope_tables(positions)
    lb_all = hgrn_lower_bounds(hg_lb)
    c_act = jax.nn.silu(c)
    for i in range(DEPTH):
        mod = c_act @ ada_w[i] + ada_b[i]
        sh1, sc1, g1, sh2, sc2, g2 = jnp.split(mod, 6, axis=-1)
        h = modulate(rmsnorm(x, mix_norm_g[i]), sh1, sc1)
        kind, j = i % N_MIXERS, i // N_MIXERS
        if kind == 0:
            y = gmlp_mixer(h, gm_w_in[j], gm_ln_g[j], gm_ln_b[j], gm_w_s[j], gm_b_s[j], gm_w_out[j])
        elif kind == 1:
            y = mla_mixer(h, cos, sin, mla_w_a[j], mla_q_norm_g[j], mla_kv_norm_g[j],
                          mla_w_qb[j], mla_w_kvb[j], mla_w_o[j])
        else:
            y = hgrn2_mixer(h, lb_all[i].astype(h.dtype), hg_w_in[j], hg_norm_g[j], hg_w_o[j])
        x = x + (1.0 + g1)[:, None, :] * y
        h = modulate(rmsnorm(x, ffn_norm_g[i]), sh2, sc2)
        x = x + (1.0 + g2)[:, None, :] * conv_ffn(h, ff_w_up[i], ff_conv_w[i], ff_conv_b[i], ff_w_down[i])
    return rmsnorm(x, final_g)
```

```python
import functools

import jax
import jax.numpy as jnp
from jax import lax
from jax.experimental import pallas as pl
from jax.experimental.pallas import tpu as pltpu

F32 = jnp.float32
BF16 = jnp.bfloat16

D_MODEL = 1024
DEPTH = 4
N_MIXERS = 3
EPS = 1e-6

GM_CHUNK = 128
GM_GROUPS = 8
GM_INNER = 2 * D_MODEL
GM_GROUP_DIM = GM_INNER // GM_GROUPS

MLA_HEADS = 8
MLA_Q_RANK = D_MODEL // 4
MLA_KV_RANK = D_MODEL // 8
MLA_NOPE = 128
MLA_ROPE = 64
MLA_V = 128
ROPE_THETA = 10000.0

HG_HEADS = 8
HG_DK = 128
HG_DV = D_MODEL // HG_HEADS

FF_DIM = 2816
CONV_W = 3

LANES = 128
BF16_ROWS = 16
VMEM_LIMIT = 56 * 1024 * 1024

TM_FFN = 512
FC_FFN = 256
TM_GMLP = 256
TM_PROJ = 512
TQ_ATTN = 512
TM_OUT = 512
TB_HGRN = 256
HG_KCHUNK = 128
MLA_HEAD_PAD = MLA_NOPE + LANES


def _params(sem):
    return pltpu.CompilerParams(dimension_semantics=sem, vmem_limit_bytes=VMEM_LIMIT)


def _rms(x, g):
    return x * lax.rsqrt(jnp.mean(x * x, axis=-1, keepdims=True) + EPS) * g


def _rms_mod(x, g, shift, scale):
    return _rms(x, g) * (1.0 + scale) + shift


def _dot(a, b):
    return jnp.dot(a, b, preferred_element_type=F32)


def _dot_nt(a, b):
    return lax.dot_general(a, b, (((1,), (1,)), ((), ())), preferred_element_type=F32)


def _dot_tn(a, b):
    return lax.dot_general(a, b, (((0,), (0,)), ((), ())), preferred_element_type=F32)


def _silu(x):
    return x * jax.nn.sigmoid(x)


def _adaln_kernel(c_ref, w_ref, b_ref, o_ref):
    c_act = _silu(c_ref[...]).astype(BF16)
    o_ref[0] = _dot(c_act, w_ref[0].astype(BF16)) + b_ref[0]


def _adaln(c, ada_w, ada_b):
    batch = c.shape[0]
    rows = 8
    tn = 1536
    c_pad = jnp.pad(c, ((0, rows - batch), (0, 0)))
    n = 6 * D_MODEL
    mod = pl.pallas_call(
        _adaln_kernel,
        out_shape=jax.ShapeDtypeStruct((DEPTH, rows, n), F32),
        grid=(DEPTH, n // tn),
        in_specs=[
            pl.BlockSpec((rows, D_MODEL), lambda i, j: (0, 0)),
            pl.BlockSpec((1, D_MODEL, tn), lambda i, j: (i, 0, j)),
            pl.BlockSpec((1, 1, tn), lambda i, j: (i, 0, j)),
        ],
        out_specs=pl.BlockSpec((1, rows, tn), lambda i, j: (i, 0, j)),
        compiler_params=_params(("parallel", "parallel")),
        name="adaln_mod",
    )(c_pad, ada_w, ada_b.reshape(DEPTH, 1, n))
    return mod[:, :batch].reshape(DEPTH, batch, 6, D_MODEL)


def _ffn_kernel(x_ref, xh_ref, mod_ref, ng_ref, wg_ref, wv_ref, cwg_ref, cwv_ref, cbg_ref, cbv_ref,
                wd_ref, fg_ref, o_ref, h_scr, acc_scr, *, final_norm):
    s_idx = pl.program_id(1)
    j = pl.program_id(2)
    mod = mod_ref[0]
    halo = BF16_ROWS

    @pl.when(j == 0)
    def _():
        g = ng_ref[...]
        shift, scale = mod[3:4], mod[4:5]
        h_halo = jnp.where(s_idx == 0, 0.0, _rms_mod(xh_ref[0], g, shift, scale))
        h_scr[0:halo] = h_halo.astype(BF16)
        h_scr[halo:] = _rms_mod(x_ref[0], g, shift, scale).astype(BF16)
        acc_scr[...] = jnp.zeros_like(acc_scr)

    h = h_scr[...]

    def conv(a, cw_ref, cb_ref):
        cw = cw_ref[...]
        return (cb_ref[...] + cw[2:3] * a[halo:] + cw[1:2] * pltpu.roll(a, 1, 0)[halo:]
                + cw[0:1] * pltpu.roll(a, 2, 0)[halo:])

    yg = conv(_dot(h, wg_ref[...]), cwg_ref, cbg_ref)
    yv = conv(_dot(h, wv_ref[...]), cwv_ref, cbv_ref)
    act = (_silu(yg) * yv).astype(BF16)
    acc_scr[...] += _dot(act, wd_ref[...])

    @pl.when(j == pl.num_programs(2) - 1)
    def _():
        out = x_ref[0] + (1.0 + mod[5:6]) * acc_scr[...]
        if final_norm:
            out = _rms(out, fg_ref[...])
        o_ref[0] = out


def _conv_ffn(x, mod, norm_g, w_up, conv_w, conv_b, w_down, final_g, final_norm):
    batch, seq, d = x.shape
    tm, fc, halo = TM_FFN, FC_FFN, BF16_ROWS
    nf = FF_DIM // fc
    halo_blocks = tm // halo
    return pl.pallas_call(
        functools.partial(_ffn_kernel, final_norm=final_norm),
        out_shape=jax.ShapeDtypeStruct(x.shape, F32),
        grid=(batch, seq // tm, nf),
        in_specs=[
            pl.BlockSpec((1, tm, d), lambda b, s, j: (b, s, 0)),
            pl.BlockSpec((1, halo, d), lambda b, s, j: (b, jnp.maximum(s * halo_blocks - 1, 0), 0)),
            pl.BlockSpec((1, 6, d), lambda b, s, j: (b, 0, 0)),
            pl.BlockSpec((1, d), lambda b, s, j: (0, 0)),
            pl.BlockSpec((d, fc), lambda b, s, j: (0, j)),
            pl.BlockSpec((d, fc), lambda b, s, j: (0, nf + j)),
            pl.BlockSpec((CONV_W, fc), lambda b, s, j: (0, j)),
            pl.BlockSpec((CONV_W, fc), lambda b, s, j: (0, nf + j)),
            pl.BlockSpec((1, fc), lambda b, s, j: (0, j)),
            pl.BlockSpec((1, fc), lambda b, s, j: (0, nf + j)),
            pl.BlockSpec((fc, d), lambda b, s, j: (j, 0)),
            pl.BlockSpec((1, d), lambda b, s, j: (0, 0)),
        ],
        out_specs=pl.BlockSpec((1, tm, d), lambda b, s, j: (b, s, 0)),
        scratch_shapes=[pltpu.VMEM((tm + halo, d), BF16), pltpu.VMEM((tm, d), F32)],
        compiler_params=_params(("parallel", "parallel", "arbitrary")),
        name="conv_ffn",
    )(x, x, mod, norm_g.reshape(1, d), w_up, w_up, conv_w, conv_w, conv_b.reshape(1, -1),
      conv_b.reshape(1, -1), w_down, final_g.reshape(1, d))


def _gmlp_kernel(x_ref, mod_ref, ng_ref, win_ref, lng_ref, lnb_ref, ws_ref, bs_ref, wout_ref, o_ref, v_scr):
    mod = mod_ref[0]
    x = x_ref[0]
    tm = x.shape[0]
    h = _rms_mod(x, ng_ref[...], mod[0:1], mod[1:2]).astype(BF16)

    v = jax.nn.gelu(_dot(h, win_ref[:, GM_INNER:]))
    mu = jnp.mean(v, axis=-1, keepdims=True)
    vc = v - mu
    var = jnp.mean(vc * vc, axis=-1, keepdims=True)
    v_scr[...] = (vc * lax.rsqrt(var + EPS) * lng_ref[...] + lnb_ref[...]).astype(BF16)

    row = lax.broadcasted_iota(jnp.int32, (GM_CHUNK, GM_CHUNK), 0)
    col = lax.broadcasted_iota(jnp.int32, (GM_CHUNK, GM_CHUNK), 1)
    causal = row >= col

    acc = jnp.zeros((tm, D_MODEL), F32)
    for g in range(GM_GROUPS):
        lo, hi = g * GM_GROUP_DIM, (g + 1) * GM_GROUP_DIM
        u = jax.nn.gelu(_dot(h, win_ref[:, lo:hi]))
        w_s = jnp.where(causal, ws_ref[g], 0.0).astype(BF16)
        bias = bs_ref[:, lo:hi]
        sv = jnp.concatenate(
            [_dot(w_s, v_scr[c * GM_CHUNK:(c + 1) * GM_CHUNK, lo:hi]) + bias for c in range(tm // GM_CHUNK)],
            axis=0)
        acc = acc + _dot((u * sv).astype(BF16), wout_ref[lo:hi, :])
    o_ref[0] = x + (1.0 + mod[2:3]) * acc


def _gmlp_mixer(x, mod, norm_g, w_in, ln_g, ln_b, w_s, b_s, w_out):
    batch, seq, d = x.shape
    tm = TM_GMLP
    bias = jnp.repeat(b_s.T, GM_GROUP_DIM, axis=1)
    const = lambda b, s: (0, 0)
    return pl.pallas_call(
        _gmlp_kernel,
        out_shape=jax.ShapeDtypeStruct(x.shape, F32),
        grid=(batch, seq // tm),
        in_specs=[
            pl.BlockSpec((1, tm, d), lambda b, s: (b, s, 0)),
            pl.BlockSpec((1, 6, d), lambda b, s: (b, 0, 0)),
            pl.BlockSpec((1, d), const),
            pl.BlockSpec((d, 2 * GM_INNER), const),
            pl.BlockSpec((1, GM_INNER), const),
            pl.BlockSpec((1, GM_INNER), const),
            pl.BlockSpec((GM_GROUPS, GM_CHUNK, GM_CHUNK), lambda b, s: (0, 0, 0)),
            pl.BlockSpec((GM_CHUNK, GM_INNER), const),
            pl.BlockSpec((GM_INNER, d), const),
        ],
        out_specs=pl.BlockSpec((1, tm, d), lambda b, s: (b, s, 0)),
        scratch_shapes=[pltpu.VMEM((tm, GM_INNER), BF16)],
        compiler_params=_params(("parallel", "parallel")),
        name="gmlp_mixer",
    )(x, mod, norm_g.reshape(1, d), w_in, ln_g.reshape(1, -1), ln_b.reshape(1, -1), w_s, bias, w_out)


def _out_proj_kernel(a_ref, x_ref, mod_ref, w_ref, o_ref):
    if len(a_ref.shape) == 4:
        a = jnp.concatenate([a_ref[0, hd] for hd in range(a_ref.shape[1])], axis=-1)
    else:
        a = a_ref[0]
    o_ref[0] = x_ref[0] + (1.0 + mod_ref[0][2:3]) * _dot(a, w_ref[...])


def _out_proj(a, x, mod, w):
    batch, seq, d = x.shape
    k = w.shape[0]
    tm = TM_OUT
    if a.ndim == 4:
        a_spec = pl.BlockSpec((1, a.shape[1], tm, a.shape[3]), lambda b, s: (b, 0, s, 0))
    else:
        a_spec = pl.BlockSpec((1, tm, k), lambda b, s: (b, s, 0))
    return pl.pallas_call(
        _out_proj_kernel,
        out_shape=jax.ShapeDtypeStruct(x.shape, F32),
        grid=(batch, seq // tm),
        in_specs=[
            a_spec,
            pl.BlockSpec((1, tm, d), lambda b, s: (b, s, 0)),
            pl.BlockSpec((1, 6, d), lambda b, s: (b, 0, 0)),
            pl.BlockSpec((k, d), lambda b, s: (0, 0)),
        ],
        out_specs=pl.BlockSpec((1, tm, d), lambda b, s: (b, s, 0)),
        compiler_params=_params(("parallel", "parallel")),
        name="out_proj",
    )(a, x, mod, w)


def _mla_proj_kernel(x_ref, pos_ref, mod_ref, ng_ref, invf_ref, wa_ref, qg_ref, kvg_ref, wq_ref, wqs_ref,
                     wk_ref, wv_ref, q_ref, k_ref, v_ref, *, scale):
    mod = mod_ref[0]
    h = _rms_mod(x_ref[0], ng_ref[...], mod[0:1], mod[1:2]).astype(BF16)
    a = _dot(h, wa_ref[...])
    cq = _rms(a[:, :MLA_Q_RANK], qg_ref[...]).astype(BF16)
    kv_lo = MLA_Q_RANK
    ckv = _rms(a[:, kv_lo:kv_lo + MLA_KV_RANK], kvg_ref[...]).astype(BF16)
    kr_lo = kv_lo + MLA_KV_RANK
    k_rope, k_rope_sw = a[:, kr_lo:kr_lo + LANES], a[:, kr_lo + LANES:kr_lo + 2 * LANES]

    ang = pos_ref[0] * invf_ref[...]
    cos, sin = jnp.cos(ang), jnp.sin(ang)

    q_main = _dot(cq, wq_ref[...])
    q_sw = _dot(cq, wqs_ref[...])
    k_nope = _dot(ckv, wk_ref[...])
    v_ref[0] = _dot(ckv, wv_ref[...]).astype(BF16)
    k_r = (k_rope * cos + k_rope_sw * sin).astype(BF16)
    for hd in range(MLA_HEADS):
        lo = hd * MLA_HEAD_PAD
        mid = lo + MLA_NOPE
        q_ref[0, :, lo:mid] = (q_main[:, lo:mid] * scale).astype(BF16)
        q_r = q_main[:, mid:mid + LANES] * cos + q_sw[:, hd * LANES:(hd + 1) * LANES] * sin
        q_ref[0, :, mid:mid + LANES] = (q_r * scale).astype(BF16)
        k_ref[0, :, lo:mid] = k_nope[:, hd * MLA_NOPE:(hd + 1) * MLA_NOPE].astype(BF16)
        k_ref[0, :, mid:mid + LANES] = k_r


def _flash_kernel(q_ref, k_ref, v_ref, o_ref, m_scr, l_scr, acc_scr):
    qi = pl.program_id(2)
    kj = pl.program_id(3)
    tq = q_ref.shape[1]
    tk = k_ref.shape[1]

    @pl.when(kj == 0)
    def _():
        m_scr[...] = jnp.full_like(m_scr, -jnp.inf)
        l_scr[...] = jnp.zeros_like(l_scr)
        acc_scr[...] = jnp.zeros_like(acc_scr)

    def step(masked):
        s = _dot_nt(q_ref[0], k_ref[0])
        if masked:
            row = lax.broadcasted_iota(jnp.int32, (tq, tk), 0)
            col = lax.broadcasted_iota(jnp.int32, (tq, tk), 1)
            s = jnp.where(row >= col, s, -jnp.inf)
        m_prev = m_scr[...]
        m_next = jnp.maximum(m_prev, jnp.max(s, axis=1, keepdims=True))
        p = jnp.exp(s - pltpu.repeat(m_next, tk // LANES, axis=1))
        alpha = jnp.exp(m_prev - m_next)
        l_scr[...] = alpha * l_scr[...] + jnp.sum(p, axis=1, keepdims=True)
        acc_scr[...] = alpha * acc_scr[...] + _dot(p.astype(BF16), v_ref[0])
        m_scr[...] = m_next

    pl.when(kj < qi)(functools.partial(step, False))

    @pl.when(kj == qi)
    def _():
        step(True)
        o_ref[0] = (acc_scr[...] / l_scr[...]).astype(BF16)


def _mla_mixer(x, positions, mod, norm_g, w_a, q_norm_g, kv_norm_g, w_qb, w_kvb, w_o):
    batch, seq, d = x.shape
    heads, rope, half = MLA_HEADS, MLA_ROPE, MLA_ROPE // 2
    pad = LANES - rope

    def swap_halves(w):
        return jnp.concatenate([-w[..., half:], w[..., :half]], axis=-1)

    def pad_lanes(w):
        return jnp.pad(w, [(0, 0)] * (w.ndim - 1) + [(0, pad)])

    kr_lo = MLA_Q_RANK + MLA_KV_RANK
    w_kr = w_a[:, kr_lo:]
    w_a_ext = jnp.concatenate([w_a[:, :kr_lo], pad_lanes(w_kr), pad_lanes(swap_halves(w_kr))], axis=1)

    wq = w_qb.reshape(MLA_Q_RANK, heads, MLA_NOPE + rope)
    wq_rope = wq[:, :, MLA_NOPE:]
    w_q_main = jnp.concatenate([wq[:, :, :MLA_NOPE], pad_lanes(wq_rope)], axis=-1).reshape(MLA_Q_RANK, -1)
    w_q_sw = pad_lanes(swap_halves(wq_rope)).reshape(MLA_Q_RANK, -1)

    wkv = w_kvb.reshape(MLA_KV_RANK, heads, MLA_NOPE + MLA_V)
    w_k = wkv[:, :, :MLA_NOPE].reshape(MLA_KV_RANK, -1)
    w_v = wkv[:, :, MLA_NOPE:].reshape(MLA_KV_RANK, -1)

    inv_freq = ROPE_THETA ** (-jnp.arange(0, rope, 2, dtype=F32) / rope)
    inv_freq = jnp.tile(inv_freq, LANES // half).reshape(1, LANES)
    pos = positions.astype(F32).reshape(batch, seq, 1)

    tm = TM_PROJ
    qk_w = heads * MLA_HEAD_PAD
    const = lambda b, s: (0, 0)
    full = lambda arr: pl.BlockSpec(arr.shape, const)
    weights = [w_a_ext.astype(BF16), q_norm_g.reshape(1, -1), kv_norm_g.reshape(1, -1), w_q_main.astype(BF16),
               w_q_sw.astype(BF16), w_k.astype(BF16), w_v.astype(BF16)]
    q, k, v = pl.pallas_call(
        functools.partial(_mla_proj_kernel, scale=(MLA_NOPE + rope) ** -0.5),
        out_shape=(jax.ShapeDtypeStruct((batch, seq, qk_w), BF16),
                   jax.ShapeDtypeStruct((batch, seq, qk_w), BF16),
                   jax.ShapeDtypeStruct((batch, seq, heads * MLA_V), BF16)),
        grid=(batch, seq // tm),
        in_specs=[
            pl.BlockSpec((1, tm, d), lambda b, s: (b, s, 0)),
            pl.BlockSpec((1, tm, 1), lambda b, s: (b, s, 0)),
            pl.BlockSpec((1, 6, d), lambda b, s: (b, 0, 0)),
            pl.BlockSpec((1, d), const),
            pl.BlockSpec((1, LANES), const),
        ] + [full(w) for w in weights],
        out_specs=(pl.BlockSpec((1, tm, qk_w), lambda b, s: (b, s, 0)),
                   pl.BlockSpec((1, tm, qk_w), lambda b, s: (b, s, 0)),
                   pl.BlockSpec((1, tm, heads * MLA_V), lambda b, s: (b, s, 0))),
        compiler_params=_params(("parallel", "parallel")),
        name="mla_proj",
    )(x, pos, mod, norm_g.reshape(1, d), inv_freq, *weights)

    t = TQ_ATTN
    nblk = seq // t
    o = pl.pallas_call(
        _flash_kernel,
        out_shape=jax.ShapeDtypeStruct((batch, seq, heads * MLA_V), BF16),
        grid=(batch, heads, nblk, nblk),
        in_specs=[
            pl.BlockSpec((1, t, MLA_HEAD_PAD), lambda b, h, i, j: (b, i, h)),
            pl.BlockSpec((1, t, MLA_HEAD_PAD), lambda b, h, i, j: (b, jnp.minimum(i, j), h)),
            pl.BlockSpec((1, t, MLA_V), lambda b, h, i, j: (b, jnp.minimum(i, j), h)),
        ],
        out_specs=pl.BlockSpec((1, t, MLA_V), lambda b, h, i, j: (b, i, h)),
        scratch_shapes=[pltpu.VMEM((t, LANES), F32), pltpu.VMEM((t, LANES), F32), pltpu.VMEM((t, MLA_V), F32)],
        compiler_params=_params(("parallel", "parallel", "parallel", "arbitrary")),
        name="mla_flash",
    )(q, k, v)
    return _out_proj(o, x, mod, w_o.astype(BF16))


def _hgrn_proj_kernel(x_ref, mod_ref, ng_ref, lbp_ref, w_ref, q_ref, k_ref, lf_ref, v_ref, sg_ref, *, layer):
    mod = mod_ref[0]
    h = _rms_mod(x_ref[0], ng_ref[...], mod[0:1], mod[1:2]).astype(BF16)
    nk = HG_HEADS * HG_DK
    nv = HG_HEADS * HG_DV

    lbp = lbp_ref[...]
    e = jnp.exp(lbp - jnp.max(lbp, axis=0, keepdims=True))
    p = e / jnp.sum(e, axis=0, keepdims=True)
    lb = jnp.sum(p[1:layer + 1], axis=0, keepdims=True) if layer >= 1 else jnp.zeros((1, nk), F32)

    def store_heads(ref, val, width):
        for hd in range(HG_HEADS):
            ref[0, hd] = val[:, hd * width:(hd + 1) * width]

    store_heads(q_ref, _dot(h, w_ref[:, :nk]).astype(BF16), HG_DK)
    f = lb + (1.0 - lb) * jax.nn.sigmoid(_dot(h, w_ref[:, nk:2 * nk]))
    store_heads(lf_ref, jnp.log(f), HG_DK)
    store_heads(k_ref, (1.0 - f).astype(BF16), HG_DK)
    store_heads(v_ref, _dot(h, w_ref[:, 2 * nk:2 * nk + nv]).astype(BF16), HG_DV)
    store_heads(sg_ref, _silu(_dot(h, w_ref[:, 2 * nk + nv:])).astype(BF16), HG_DV)


def _hgrn_chunk(q, kk, g, v, state_t, b_scr):
    c_rows = q.shape[0]
    row = lax.broadcasted_iota(jnp.int32, (c_rows, HG_DK), 0)
    trow = lax.broadcasted_iota(jnp.int32, (c_rows, c_rows), 0)
    tcol = lax.broadcasted_iota(jnp.int32, (c_rows, c_rows), 1)

    b = g
    sh = 1
    while sh < c_rows:
        b = b + jnp.where(row >= sh, pltpu.roll(b, sh, 0), 0.0)
        sh *= 2
    b_scr[...] = b

    attn = jnp.where(trow == tcol, _dot_nt(q.astype(BF16), kk.astype(BF16)), 0.0)
    half = 1
    while half < c_rows:
        right = (row & half) != 0
        if half == 1:
            arg = jnp.where(right, g, 0.0)
        elif half == 2:
            phase = row & 3
            arg = jnp.where(phase == 0, pltpu.roll(g, c_rows - 1, 0),
                            jnp.where(phase == 1, 0.0,
                                      jnp.where(phase == 2, g, g + pltpu.roll(g, 1, 0))))
        else:
            blk = 2 * half
            ref_rows = jnp.concatenate(
                [jnp.broadcast_to(b_scr[m * blk + half - 1:m * blk + half, :], (blk, HG_DK))
                 for m in range(c_rows // blk)], axis=0)
            diff = b - ref_rows
            arg = jnp.where(right, diff, -diff)
        x = jnp.exp(arg)
        q_l = jnp.where(right, q * x, 0.0).astype(BF16)
        k_l = jnp.where(right, 0.0, kk * x).astype(BF16)
        same_block = (trow & -(2 * half)) == (tcol & -(2 * half))
        attn = attn + jnp.where(same_block, _dot_nt(q_l, k_l), 0.0)
        half *= 2

    b_end = b_scr[c_rows - 1:c_rows, :]
    o = _dot(attn.astype(BF16), v.astype(BF16))
    o = o + _dot_nt((q * jnp.exp(b)).astype(BF16), state_t.astype(BF16))
    k_dec = (kk * jnp.exp(b_end - b)).astype(BF16)
    new_state_t = jnp.exp(b_end) * state_t + _dot_tn(v.astype(BF16), k_dec)
    return o, new_state_t


def _hgrn_rec_kernel(q_ref, k_ref, lf_ref, v_ref, sg_ref, g_ref, o_ref, state_scr, b_scr):
    @pl.when(pl.program_id(1) == 0)
    def _():
        state_scr[...] = jnp.zeros_like(state_scr)

    norm_g = g_ref[...]
    tb = q_ref.shape[2]

    def head(hd, carry):
        state_t = state_scr[hd]
        for c in range(tb // HG_KCHUNK):
            rows = pl.ds(c * HG_KCHUNK, HG_KCHUNK)
            o, state_t = _hgrn_chunk(q_ref[0, hd, rows, :].astype(F32), k_ref[0, hd, rows, :].astype(F32),
                                     lf_ref[0, hd, rows, :], v_ref[0, hd, rows, :].astype(F32), state_t, b_scr)
            o_ref[0, hd, rows, :] = (_rms(o, norm_g) * sg_ref[0, hd, rows, :].astype(F32)).astype(BF16)
        state_scr[hd] = state_t
        return carry

    lax.fori_loop(0, HG_HEADS, head, 0)


def _hgrn_mixer(x, mod, norm_g, hg_lb, layer, w_in, out_norm_g, w_o):
    batch, seq, d = x.shape
    nk = HG_HEADS * HG_DK
    nv = HG_HEADS * HG_DV
    tm = TM_PROJ
    const = lambda b, s: (0, 0)
    heads_shape = lambda w, dt: jax.ShapeDtypeStruct((batch, HG_HEADS, seq, w), dt)
    heads_blk = lambda rows, w: pl.BlockSpec((1, HG_HEADS, rows, w), lambda b, s: (b, 0, s, 0))
    q, k, lf, v, sg = pl.pallas_call(
        functools.partial(_hgrn_proj_kernel, layer=layer),
        out_shape=(heads_shape(HG_DK, BF16), heads_shape(HG_DK, BF16), heads_shape(HG_DK, F32),
                   heads_shape(HG_DV, BF16), heads_shape(HG_DV, BF16)),
        grid=(batch, seq // tm),
        in_specs=[
            pl.BlockSpec((1, tm, d), lambda b, s: (b, s, 0)),
            pl.BlockSpec((1, 6, d), lambda b, s: (b, 0, 0)),
            pl.BlockSpec((1, d), const),
            pl.BlockSpec((DEPTH, nk), const),
            pl.BlockSpec(w_in.shape, const),
        ],
        out_specs=(heads_blk(tm, HG_DK), heads_blk(tm, HG_DK), heads_blk(tm, HG_DK), heads_blk(tm, HG_DV),
                   heads_blk(tm, HG_DV)),
        compiler_params=_params(("parallel", "parallel")),
        name="hgrn_proj",
    )(x, mod, norm_g.reshape(1, d), hg_lb, w_in)

    tb = TB_HGRN
    o = pl.pallas_call(
        _hgrn_rec_kernel,
        out_shape=heads_shape(HG_DV, BF16),
        grid=(batch, seq // tb),
        in_specs=[heads_blk(tb, HG_DK), heads_blk(tb, HG_DK), heads_blk(tb, HG_DK), heads_blk(tb, HG_DV),
                  heads_blk(tb, HG_DV), pl.BlockSpec((1, HG_DV), const)],
        out_specs=heads_blk(tb, HG_DV),
        scratch_shapes=[pltpu.VMEM((HG_HEADS, HG_DV, HG_DK), F32), pltpu.VMEM((HG_KCHUNK, HG_DK), F32)],
        compiler_params=_params(("parallel", "arbitrary")),
        name="hgrn_recurrence",
    )(q, k, lf, v, sg, out_norm_g.reshape(1, HG_DV))
    return _out_proj(o, x, mod, w_o)


def kernel(x, c, positions, ada_w, ada_b, mix_norm_g, ffn_norm_g, gm_w_in, gm_ln_g, gm_ln_b, gm_w_s, gm_b_s,
           gm_w_out, mla_w_a, mla_q_norm_g, mla_kv_norm_g, mla_w_qb, mla_w_kvb, mla_w_o, hg_lb, hg_w_in,
           hg_norm_g, hg_w_o, ff_w_up, ff_conv_w, ff_conv_b, ff_w_down, final_g):
    mod_all = _adaln(c, ada_w, ada_b)
    for i in range(DEPTH):
        mod = mod_all[i]
        kind, j = i % N_MIXERS, i // N_MIXERS
        if kind == 0:
            x = _gmlp_mixer(x, mod, mix_norm_g[i], gm_w_in[j].astype(BF16), gm_ln_g[j], gm_ln_b[j], gm_w_s[j],
                            gm_b_s[j], gm_w_out[j].astype(BF16))
        elif kind == 1:
            x = _mla_mixer(x, positions, mod, mix_norm_g[i], mla_w_a[j], mla_q_norm_g[j], mla_kv_norm_g[j],
                           mla_w_qb[j], mla_w_kvb[j], mla_w_o[j])
        else:
            x = _hgrn_mixer(x, mod, mix_norm_g[i], hg_lb, i, hg_w_in[j].astype(BF16), hg_norm_g[j],
                            hg_w_o[j].astype(BF16))
        x = _conv_ffn(x, mod, ffn_norm_g[i], ff_w_up[i].astype(BF16), ff_conv_w[i], ff_conv_b[i],
                      ff_w_down[i].astype(BF16), final_g, final_norm=(i == DEPTH - 1))
    return x
```

```python
import functools

import jax
import jax.numpy as jnp
from jax import lax
from jax.experimental import pallas as pl
from jax.experimental.pallas import tpu as pltpu

F32 = jnp.float32
BF16 = jnp.bfloat16

D_MODEL = 1024
DEPTH = 4
N_MIXERS = 3
EPS = 1e-6

GM_CHUNK = 128
GM_GROUPS = 8
GM_INNER = 2 * D_MODEL
GM_GROUP_DIM = GM_INNER // GM_GROUPS

MLA_HEADS = 8
MLA_Q_RANK = D_MODEL // 4
MLA_KV_RANK = D_MODEL // 8
MLA_NOPE = 128
MLA_ROPE = 64
MLA_V = 128
ROPE_THETA = 10000.0

HG_HEADS = 8
HG_DK = 128
HG_DV = D_MODEL // HG_HEADS

FF_DIM = 2816
CONV_W = 3

LANES = 128
BF16_ROWS = 16
VMEM_LIMIT = 56 * 1024 * 1024

TM_FFN = 512
FC_FFN = 256
TM_GMLP = 512
TM_PROJ = 512
TQ_ATTN = 512
FLASH_HEADS = 2
TM_OUT = 512
TB_HGRN = 256
HG_KCHUNK = 128
MLA_HEAD_PAD = MLA_NOPE + LANES


def _params(sem):
    return pltpu.CompilerParams(dimension_semantics=sem, vmem_limit_bytes=VMEM_LIMIT)


def _resident(arr):
    zeros = (0,) * arr.ndim
    return pl.BlockSpec(arr.shape, lambda *_: zeros, pipeline_mode=pl.Buffered(1))


def _rms(x, g):
    return x * lax.rsqrt(jnp.mean(x * x, axis=-1, keepdims=True) + EPS) * g


def _rms_mod(x, g, shift, scale):
    return _rms(x, g) * (1.0 + scale) + shift


def _dot(a, b):
    return jnp.dot(a, b, preferred_element_type=F32)


def _dot_nt(a, b):
    return lax.dot_general(a, b, (((1,), (1,)), ((), ())), preferred_element_type=F32)


def _dot_tn(a, b):
    return lax.dot_general(a, b, (((0,), (0,)), ((), ())), preferred_element_type=F32)


def _silu(x):
    return x * jax.nn.sigmoid(x)


def _adaln_kernel(c_ref, w_ref, b_ref, o_ref):
    c_act = _silu(c_ref[...]).astype(BF16)
    o_ref[0] = _dot(c_act, w_ref[0].astype(BF16)) + b_ref[0]


def _adaln(c, ada_w, ada_b):
    batch = c.shape[0]
    rows = 8
    tn = 1536
    c_pad = jnp.pad(c, ((0, rows - batch), (0, 0)))
    n = 6 * D_MODEL
    mod = pl.pallas_call(
        _adaln_kernel,
        out_shape=jax.ShapeDtypeStruct((DEPTH, rows, n), F32),
        grid=(DEPTH, n // tn),
        in_specs=[
            pl.BlockSpec((rows, D_MODEL), lambda i, j: (0, 0)),
            pl.BlockSpec((1, D_MODEL, tn), lambda i, j: (i, 0, j)),
            pl.BlockSpec((1, 1, tn), lambda i, j: (i, 0, j)),
        ],
        out_specs=pl.BlockSpec((1, rows, tn), lambda i, j: (i, 0, j)),
        compiler_params=_params(("parallel", "parallel")),
        name="adaln_mod",
    )(c_pad, ada_w, ada_b.reshape(DEPTH, 1, n))
    return mod[:, :batch].reshape(DEPTH, batch, 6, D_MODEL)


def _ffn_kernel(x_ref, xh_ref, mod_ref, ng_ref, wu_ref, cw_ref, cb_ref, wd_ref, fg_ref, o_ref, h_scr, act_scr,
                *, final_norm):
    mod = mod_ref[0]
    halo = BF16_ROWS
    g = ng_ref[...]
    shift, scale = mod[3:4], mod[4:5]
    h_halo = jnp.where(pl.program_id(1) == 0, 0.0, _rms_mod(xh_ref[0], g, shift, scale))
    h_scr[0:halo] = h_halo.astype(BF16)
    h_scr[halo:] = _rms_mod(x_ref[0], g, shift, scale).astype(BF16)
    h = h_scr[...]

    def conv(lo):
        a = _dot(h, wu_ref[:, lo:lo + FC_FFN])
        cw = cw_ref[:, lo:lo + FC_FFN]
        return (cb_ref[:, lo:lo + FC_FFN] + cw[2:3] * a[halo:] + cw[1:2] * pltpu.roll(a, 1, 0)[halo:]
                + cw[0:1] * pltpu.roll(a, 2, 0)[halo:])

    for c in range(FF_DIM // FC_FFN):
        lo = c * FC_FFN
        act_scr[:, lo:lo + FC_FFN] = (_silu(conv(lo)) * conv(FF_DIM + lo)).astype(BF16)

    out = x_ref[0] + (1.0 + mod[5:6]) * _dot(act_scr[...], wd_ref[...])
    if final_norm:
        out = _rms(out, fg_ref[...])
    o_ref[0] = out


def _conv_ffn(x, mod, norm_g, w_up, conv_w, conv_b, w_down, final_g, final_norm):
    batch, seq, d = x.shape
    tm, halo = TM_FFN, BF16_ROWS
    halo_blocks = tm // halo
    const = lambda b, s: (0, 0)
    conv_b = conv_b.reshape(1, -1)
    return pl.pallas_call(
        functools.partial(_ffn_kernel, final_norm=final_norm),
        out_shape=jax.ShapeDtypeStruct(x.shape, F32),
        grid=(batch, seq // tm),
        in_specs=[
            pl.BlockSpec((1, tm, d), lambda b, s: (b, s, 0)),
            pl.BlockSpec((1, halo, d), lambda b, s: (b, jnp.maximum(s * halo_blocks - 1, 0), 0)),
            pl.BlockSpec((1, 6, d), lambda b, s: (b, 0, 0)),
            pl.BlockSpec((1, d), const),
            _resident(w_up),
            _resident(conv_w),
            _resident(conv_b),
            _resident(w_down),
            pl.BlockSpec((1, d), const),
        ],
        out_specs=pl.BlockSpec((1, tm, d), lambda b, s: (b, s, 0)),
        scratch_shapes=[pltpu.VMEM((tm + halo, d), BF16), pltpu.VMEM((tm, FF_DIM), BF16)],
        compiler_params=_params(("parallel", "parallel")),
        name="conv_ffn",
    )(x, x, mod, norm_g.reshape(1, d), w_up, conv_w, conv_b, w_down, final_g.reshape(1, d))


def _gmlp_kernel(x_ref, mod_ref, ng_ref, win_ref, lng_ref, lnb_ref, ws_ref, bs_ref, wout_ref, o_ref, v_scr,
                 y_scr):
    mod = mod_ref[0]
    x = x_ref[0]
    tm = x.shape[0]
    h = _rms_mod(x, ng_ref[...], mod[0:1], mod[1:2]).astype(BF16)

    v = jax.nn.gelu(_dot(h, win_ref[:, GM_INNER:]))
    mu = jnp.mean(v, axis=-1, keepdims=True)
    vc = v - mu
    var = jnp.mean(vc * vc, axis=-1, keepdims=True)
    v_scr[...] = (vc * lax.rsqrt(var + EPS) * lng_ref[...] + lnb_ref[...]).astype(BF16)

    row = lax.broadcasted_iota(jnp.int32, (GM_CHUNK, GM_CHUNK), 0)
    col = lax.broadcasted_iota(jnp.int32, (GM_CHUNK, GM_CHUNK), 1)
    causal = row >= col

    for g in range(GM_GROUPS):
        lo, hi = g * GM_GROUP_DIM, (g + 1) * GM_GROUP_DIM
        u = jax.nn.gelu(_dot(h, win_ref[:, lo:hi]))
        w_s = jnp.where(causal, ws_ref[g], 0.0).astype(BF16)
        bias = bs_ref[:, lo:hi]
        for c in range(tm // GM_CHUNK):
            rows = slice(c * GM_CHUNK, (c + 1) * GM_CHUNK)
            sv = _dot(w_s, v_scr[rows, lo:hi]) + bias
            y_scr[rows, lo:hi] = (u[rows] * sv).astype(BF16)
    o_ref[0] = x + (1.0 + mod[2:3]) * _dot(y_scr[...], wout_ref[...])


def _gmlp_mixer(x, mod, norm_g, w_in, ln_g, ln_b, w_s, b_s, w_out):
    batch, seq, d = x.shape
    tm = TM_GMLP
    bias = jnp.repeat(b_s.T, GM_GROUP_DIM, axis=1)
    const = lambda b, s: (0, 0)
    return pl.pallas_call(
        _gmlp_kernel,
        out_shape=jax.ShapeDtypeStruct(x.shape, F32),
        grid=(batch, seq // tm),
        in_specs=[
            pl.BlockSpec((1, tm, d), lambda b, s: (b, s, 0)),
            pl.BlockSpec((1, 6, d), lambda b, s: (b, 0, 0)),
            pl.BlockSpec((1, d), const),
            _resident(w_in),
            pl.BlockSpec((1, GM_INNER), const),
            pl.BlockSpec((1, GM_INNER), const),
            _resident(w_s),
            _resident(bias),
            _resident(w_out),
        ],
        out_specs=pl.BlockSpec((1, tm, d), lambda b, s: (b, s, 0)),
        scratch_shapes=[pltpu.VMEM((tm, GM_INNER), BF16), pltpu.VMEM((tm, GM_INNER), BF16)],
        compiler_params=_params(("parallel", "parallel")),
        name="gmlp_mixer",
    )(x, mod, norm_g.reshape(1, d), w_in, ln_g.reshape(1, -1), ln_b.reshape(1, -1), w_s, bias, w_out)


def _out_proj_kernel(a_ref, x_ref, mod_ref, w_ref, o_ref):
    if len(a_ref.shape) == 4:
        a = jnp.concatenate([a_ref[0, hd] for hd in range(a_ref.shape[1])], axis=-1)
    else:
        a = a_ref[0]
    o_ref[0] = x_ref[0] + (1.0 + mod_ref[0][2:3]) * _dot(a, w_ref[...])


def _out_proj(a, x, mod, w):
    batch, seq, d = x.shape
    k = w.shape[0]
    tm = TM_OUT
    if a.ndim == 4:
        a_spec = pl.BlockSpec((1, a.shape[1], tm, a.shape[3]), lambda b, s: (b, 0, s, 0))
    else:
        a_spec = pl.BlockSpec((1, tm, k), lambda b, s: (b, s, 0))
    return pl.pallas_call(
        _out_proj_kernel,
        out_shape=jax.ShapeDtypeStruct(x.shape, F32),
        grid=(batch, seq // tm),
        in_specs=[
            a_spec,
            pl.BlockSpec((1, tm, d), lambda b, s: (b, s, 0)),
            pl.BlockSpec((1, 6, d), lambda b, s: (b, 0, 0)),
            pl.BlockSpec((k, d), lambda b, s: (0, 0)),
        ],
        out_specs=pl.BlockSpec((1, tm, d), lambda b, s: (b, s, 0)),
        compiler_params=_params(("parallel", "parallel")),
        name="out_proj",
    )(a, x, mod, w)


def _mla_proj_kernel(x_ref, pos_ref, mod_ref, ng_ref, invf_ref, wa_ref, qg_ref, kvg_ref, wq_ref, wqs_ref,
                     wk_ref, wv_ref, q_ref, k_ref, v_ref, *, scale):
    mod = mod_ref[0]
    h = _rms_mod(x_ref[0], ng_ref[...], mod[0:1], mod[1:2]).astype(BF16)
    a = _dot(h, wa_ref[...])
    cq = _rms(a[:, :MLA_Q_RANK], qg_ref[...]).astype(BF16)
    kv_lo = MLA_Q_RANK
    ckv = _rms(a[:, kv_lo:kv_lo + MLA_KV_RANK], kvg_ref[...]).astype(BF16)
    kr_lo = kv_lo + MLA_KV_RANK
    k_rope, k_rope_sw = a[:, kr_lo:kr_lo + LANES], a[:, kr_lo + LANES:kr_lo + 2 * LANES]

    ang = pos_ref[0] * invf_ref[...]
    cos, sin = jnp.cos(ang), jnp.sin(ang)

    q_main = _dot(cq, wq_ref[...])
    q_sw = _dot(cq, wqs_ref[...])
    k_nope = _dot(ckv, wk_ref[...])
    v_ref[0] = _dot(ckv, wv_ref[...]).astype(BF16)
    k_r = (k_rope * cos + k_rope_sw * sin).astype(BF16)
    for hd in range(MLA_HEADS):
        lo = hd * MLA_HEAD_PAD
        mid = lo + MLA_NOPE
        q_ref[0, :, lo:mid] = (q_main[:, lo:mid] * scale).astype(BF16)
        q_r = q_main[:, mid:mid + LANES] * cos + q_sw[:, hd * LANES:(hd + 1) * LANES] * sin
        q_ref[0, :, mid:mid + LANES] = (q_r * scale).astype(BF16)
        k_ref[0, :, lo:mid] = k_nope[:, hd * MLA_NOPE:(hd + 1) * MLA_NOPE].astype(BF16)
        k_ref[0, :, mid:mid + LANES] = k_r


def _flash_kernel(q_ref, k_ref, v_ref, o_ref, m_scr, l_scr, acc_scr):
    qi = pl.program_id(2)
    t = q_ref.shape[1]

    m_scr[...] = jnp.full_like(m_scr, -jnp.inf)
    l_scr[...] = jnp.zeros_like(l_scr)
    acc_scr[...] = jnp.zeros_like(acc_scr)

    def block(j, masked):
        rows = pl.ds(pl.multiple_of(j * t, t), t)
        for hd in range(FLASH_HEADS):
            qk_lanes = slice(hd * MLA_HEAD_PAD, (hd + 1) * MLA_HEAD_PAD)
            v_lanes = slice(hd * MLA_V, (hd + 1) * MLA_V)
            s = _dot_nt(q_ref[0, :, qk_lanes], k_ref[0, rows, qk_lanes])
            if masked:
                row = lax.broadcasted_iota(jnp.int32, (t, t), 0)
                col = lax.broadcasted_iota(jnp.int32, (t, t), 1)
                s = jnp.where(row >= col, s, -jnp.inf)
            m_prev = m_scr[hd]
            m_next = jnp.maximum(m_prev, jnp.max(s, axis=1, keepdims=True))
            p = jnp.exp(s - pltpu.repeat(m_next, t // LANES, axis=1))
            alpha = jnp.exp(m_prev - m_next)
            l_scr[hd] = alpha * l_scr[hd] + jnp.sum(p, axis=1, keepdims=True)
            acc_scr[hd] = alpha * acc_scr[hd] + _dot(p.astype(BF16), v_ref[0, rows, v_lanes])
            m_scr[hd] = m_next

    def unmasked(j, carry):
        block(j, False)
        return carry

    lax.fori_loop(0, qi, unmasked, 0)
    block(qi, True)
    for hd in range(FLASH_HEADS):
        o_ref[0, :, hd * MLA_V:(hd + 1) * MLA_V] = (acc_scr[hd] / l_scr[hd]).astype(BF16)


def _mla_mixer(x, positions, mod, norm_g, w_a, q_norm_g, kv_norm_g, w_qb, w_kvb, w_o):
    batch, seq, d = x.shape
    heads, rope, half = MLA_HEADS, MLA_ROPE, MLA_ROPE // 2
    pad = LANES - rope

    def swap_halves(w):
        return jnp.concatenate([-w[..., half:], w[..., :half]], axis=-1)

    def pad_lanes(w):
        return jnp.pad(w, [(0, 0)] * (w.ndim - 1) + [(0, pad)])

    kr_lo = MLA_Q_RANK + MLA_KV_RANK
    w_kr = w_a[:, kr_lo:]
    w_a_ext = jnp.concatenate([w_a[:, :kr_lo], pad_lanes(w_kr), pad_lanes(swap_halves(w_kr))], axis=1)

    wq = w_qb.reshape(MLA_Q_RANK, heads, MLA_NOPE + rope)
    wq_rope = wq[:, :, MLA_NOPE:]
    w_q_main = jnp.concatenate([wq[:, :, :MLA_NOPE], pad_lanes(wq_rope)], axis=-1).reshape(MLA_Q_RANK, -1)
    w_q_sw = pad_lanes(swap_halves(wq_rope)).reshape(MLA_Q_RANK, -1)

    wkv = w_kvb.reshape(MLA_KV_RANK, heads, MLA_NOPE + MLA_V)
    w_k = wkv[:, :, :MLA_NOPE].reshape(MLA_KV_RANK, -1)
    w_v = wkv[:, :, MLA_NOPE:].reshape(MLA_KV_RANK, -1)

    inv_freq = ROPE_THETA ** (-jnp.arange(0, rope, 2, dtype=F32) / rope)
    inv_freq = jnp.tile(inv_freq, LANES // half).reshape(1, LANES)
    pos = positions.astype(F32).reshape(batch, seq, 1)

    tm = TM_PROJ
    qk_w = heads * MLA_HEAD_PAD
    const = lambda b, s: (0, 0)
    full = lambda arr: pl.BlockSpec(arr.shape, const)
    weights = [w_a_ext.astype(BF16), q_norm_g.reshape(1, -1), kv_norm_g.reshape(1, -1), w_q_main.astype(BF16),
               w_q_sw.astype(BF16), w_k.astype(BF16), w_v.astype(BF16)]
    q, k, v = pl.pallas_call(
        functools.partial(_mla_proj_kernel, scale=(MLA_NOPE + rope) ** -0.5),
        out_shape=(jax.ShapeDtypeStruct((batch, seq, qk_w), BF16),
                   jax.ShapeDtypeStruct((batch, seq, qk_w), BF16),
                   jax.ShapeDtypeStruct((batch, seq, heads * MLA_V), BF16)),
        grid=(batch, seq // tm),
        in_specs=[
            pl.BlockSpec((1, tm, d), lambda b, s: (b, s, 0)),
            pl.BlockSpec((1, tm, 1), lambda b, s: (b, s, 0)),
            pl.BlockSpec((1, 6, d), lambda b, s: (b, 0, 0)),
            pl.BlockSpec((1, d), const),
            pl.BlockSpec((1, LANES), const),
        ] + [full(w) for w in weights],
        out_specs=(pl.BlockSpec((1, tm, qk_w), lambda b, s: (b, s, 0)),
                   pl.BlockSpec((1, tm, qk_w), lambda b, s: (b, s, 0)),
                   pl.BlockSpec((1, tm, heads * MLA_V), lambda b, s: (b, s, 0))),
        compiler_params=_params(("parallel", "parallel")),
        name="mla_proj",
    )(x, pos, mod, norm_g.reshape(1, d), inv_freq, *weights)

    t = TQ_ATTN
    fh = FLASH_HEADS
    o = pl.pallas_call(
        _flash_kernel,
        out_shape=jax.ShapeDtypeStruct((batch, seq, heads * MLA_V), BF16),
        grid=(batch, heads // fh, seq // t),
        in_specs=[
            pl.BlockSpec((1, t, fh * MLA_HEAD_PAD), lambda b, h, i: (b, i, h)),
            pl.BlockSpec((1, seq, fh * MLA_HEAD_PAD), lambda b, h, i: (b, 0, h)),
            pl.BlockSpec((1, seq, fh * MLA_V), lambda b, h, i: (b, 0, h)),
        ],
        out_specs=pl.BlockSpec((1, t, fh * MLA_V), lambda b, h, i: (b, i, h)),
        scratch_shapes=[pltpu.VMEM((fh, t, LANES), F32), pltpu.VMEM((fh, t, LANES), F32),
                        pltpu.VMEM((fh, t, MLA_V), F32)],
        compiler_params=_params(("parallel", "parallel", "arbitrary")),
        name="mla_flash",
    )(q, k, v)
    return _out_proj(o, x, mod, w_o.astype(BF16))


def _hgrn_proj_kernel(x_ref, mod_ref, ng_ref, lbp_ref, w_ref, q_ref, k_ref, lf_ref, v_ref, sg_ref, *, layer):
    mod = mod_ref[0]
    h = _rms_mod(x_ref[0], ng_ref[...], mod[0:1], mod[1:2]).astype(BF16)
    nk = HG_HEADS * HG_DK
    nv = HG_HEADS * HG_DV

    lbp = lbp_ref[...]
    e = jnp.exp(lbp - jnp.max(lbp, axis=0, keepdims=True))
    p = e / jnp.sum(e, axis=0, keepdims=True)
    lb = jnp.sum(p[1:layer + 1], axis=0, keepdims=True) if layer >= 1 else jnp.zeros((1, nk), F32)

    def store_heads(ref, val, width):
        for hd in range(HG_HEADS):
            ref[0, hd] = val[:, hd * width:(hd + 1) * width]

    store_heads(q_ref, _dot(h, w_ref[:, :nk]).astype(BF16), HG_DK)
    f = lb + (1.0 - lb) * jax.nn.sigmoid(_dot(h, w_ref[:, nk:2 * nk]))
    store_heads(lf_ref, jnp.log(f), HG_DK)
    store_heads(k_ref, (1.0 - f).astype(BF16), HG_DK)
    store_heads(v_ref, _dot(h, w_ref[:, 2 * nk:2 * nk + nv]).astype(BF16), HG_DV)
    store_heads(sg_ref, _silu(_dot(h, w_ref[:, 2 * nk + nv:])).astype(BF16), HG_DV)


def _hgrn_chunk(q, kk, g, v, state_t, b_scr):
    c_rows = q.shape[0]
    row = lax.broadcasted_iota(jnp.int32, (c_rows, HG_DK), 0)
    trow = lax.broadcasted_iota(jnp.int32, (c_rows, c_rows), 0)
    tcol = lax.broadcasted_iota(jnp.int32, (c_rows, c_rows), 1)

    b = g
    sh = 1
    while sh < c_rows:
        b = b + jnp.where(row >= sh, pltpu.roll(b, sh, 0), 0.0)
        sh *= 2
    b_scr[...] = b

    attn = jnp.where(trow == tcol, _dot_nt(q.astype(BF16), kk.astype(BF16)), 0.0)
    half = 1
    while half < c_rows:
        right = (row & half) != 0
        if half == 1:
            arg = jnp.where(right, g, 0.0)
        elif half == 2:
            phase = row & 3
            arg = jnp.where(phase == 0, pltpu.roll(g, c_rows - 1, 0),
                            jnp.where(phase == 1, 0.0,
                                      jnp.where(phase == 2, g, g + pltpu.roll(g, 1, 0))))
        else:
            blk = 2 * half
            ref_rows = jnp.concatenate(
                [jnp.broadcast_to(b_scr[m * blk + half - 1:m * blk + half, :], (blk, HG_DK))
                 for m in range(c_rows // blk)], axis=0)
            diff = b - ref_rows
            arg = jnp.where(right, diff, -diff)
        x = jnp.exp(arg)
        q_l = jnp.where(right, q * x, 0.0).astype(BF16)
        k_l = jnp.where(right, 0.0, kk * x).astype(BF16)
        same_block = (trow & -(2 * half)) == (tcol & -(2 * half))
        attn = attn + jnp.where(same_block, _dot_nt(q_l, k_l), 0.0)
        half *= 2

    b_end = b_scr[c_rows - 1:c_rows, :]
    o = _dot(attn.astype(BF16), v.astype(BF16))
    o = o + _dot_nt((q * jnp.exp(b)).astype(BF16), state_t.astype(BF16))
    k_dec = (kk * jnp.exp(b_end - b)).astype(BF16)
    new_state_t = jnp.exp(b_end) * state_t + _dot_tn(v.astype(BF16), k_dec)
    return o, new_state_t


def _hgrn_rec_kernel(q_ref, k_ref, lf_ref, v_ref, sg_ref, g_ref, o_ref, state_scr, b_scr):
    @pl.when(pl.program_id(1) == 0)
    def _():
        state_scr[...] = jnp.zeros_like(state_scr)

    norm_g = g_ref[...]
    tb = q_ref.shape[2]

    def head(hd, carry):
        state_t = state_scr[hd]
        for c in range(tb // HG_KCHUNK):
            rows = pl.ds(c * HG_KCHUNK, HG_KCHUNK)
            o, state_t = _hgrn_chunk(q_ref[0, hd, rows, :].astype(F32), k_ref[0, hd, rows, :].astype(F32),
                                     lf_ref[0, hd, rows, :], v_ref[0, hd, rows, :].astype(F32), state_t, b_scr)
            o_ref[0, hd, rows, :] = (_rms(o, norm_g) * sg_ref[0, hd, rows, :].astype(F32)).astype(BF16)
        state_scr[hd] = state_t
        return carry

    lax.fori_loop(0, HG_HEADS, head, 0)


def _hgrn_mixer(x, mod, norm_g, hg_lb, layer, w_in, out_norm_g, w_o):
    batch, seq, d = x.shape
    nk = HG_HEADS * HG_DK
    nv = HG_HEADS * HG_DV
    tm = TM_PROJ
    const = lambda b, s: (0, 0)
    heads_shape = lambda w, dt: jax.ShapeDtypeStruct((batch, HG_HEADS, seq, w), dt)
    heads_blk = lambda rows, w: pl.BlockSpec((1, HG_HEADS, rows, w), lambda b, s: (b, 0, s, 0))
    q, k, lf, v, sg = pl.pallas_call(
        functools.partial(_hgrn_proj_kernel, layer=layer),
        out_shape=(heads_shape(HG_DK, BF16), heads_shape(HG_DK, BF16), heads_shape(HG_DK, F32),
                   heads_shape(HG_DV, BF16), heads_shape(HG_DV, BF16)),
        grid=(batch, seq // tm),
        in_specs=[
            pl.BlockSpec((1, tm, d), lambda b, s: (b, s, 0)),
            pl.BlockSpec((1, 6, d), lambda b, s: (b, 0, 0)),
            pl.BlockSpec((1, d), const),
            pl.BlockSpec((DEPTH, nk), const),
            pl.BlockSpec(w_in.shape, const),
        ],
        out_specs=(heads_blk(tm, HG_DK), heads_blk(tm, HG_DK), heads_blk(tm, HG_DK), heads_blk(tm, HG_DV),
                   heads_blk(tm, HG_DV)),
        compiler_params=_params(("parallel", "parallel")),
        name="hgrn_proj",
    )(x, mod, norm_g.reshape(1, d), hg_lb, w_in)

    tb = TB_HGRN
    o = pl.pallas_call(
        _hgrn_rec_kernel,
        out_shape=heads_shape(HG_DV, BF16),
        grid=(batch, seq // tb),
        in_specs=[heads_blk(tb, HG_DK), heads_blk(tb, HG_DK), heads_blk(tb, HG_DK), heads_blk(tb, HG_DV),
                  heads_blk(tb, HG_DV), pl.BlockSpec((1, HG_DV), const)],
        out_specs=heads_blk(tb, HG_DV),
        scratch_shapes=[pltpu.VMEM((HG_HEADS, HG_DV, HG_DK), F32), pltpu.VMEM((HG_KCHUNK, HG_DK), F32)],
        compiler_params=_params(("parallel", "arbitrary")),
        name="hgrn_recurrence",
    )(q, k, lf, v, sg, out_norm_g.reshape(1, HG_DV))
    return _out_proj(o, x, mod, w_o)


def kernel(x, c, positions, ada_w, ada_b, mix_norm_g, ffn_norm_g, gm_w_in, gm_ln_g, gm_ln_b, gm_w_s, gm_b_s,
           gm_w_out, mla_w_a, mla_q_norm_g, mla_kv_norm_g, mla_w_qb, mla_w_kvb, mla_w_o, hg_lb, hg_w_in,
           hg_norm_g, hg_w_o, ff_w_up, ff_conv_w, ff_conv_b, ff_w_down, final_g):
    mod_all = _adaln(c, ada_w, ada_b)
    for i in range(DEPTH):
        mod = mod_all[i]
        kind, j = i % N_MIXERS, i // N_MIXERS
        if kind == 0:
            x = _gmlp_mixer(x, mod, mix_norm_g[i], gm_w_in[j].astype(BF16), gm_ln_g[j], gm_ln_b[j], gm_w_s[j],
                            gm_b_s[j], gm_w_out[j].astype(BF16))
        elif kind == 1:
            x = _mla_mixer(x, positions, mod, mix_norm_g[i], mla_w_a[j], mla_q_norm_g[j], mla_kv_norm_g[j],
                           mla_w_qb[j], mla_w_kvb[j], mla_w_o[j])
        else:
            x = _hgrn_mixer(x, mod, mix_norm_g[i], hg_lb, i, hg_w_in[j].astype(BF16), hg_norm_g[j],
                            hg_w_o[j].astype(BF16))
        x = _conv_ffn(x, mod, ffn_norm_g[i], ff_w_up[i].astype(BF16), ff_conv_w[i], ff_conv_b[i],
                      ff_w_down[i].astype(BF16), final_g, final_norm=(i == DEPTH - 1))
    return x
```

```python
import functools

import jax
import jax.numpy as jnp
from jax import lax
from jax.experimental import pallas as pl
from jax.experimental.pallas import tpu as pltpu

F32 = jnp.float32
BF16 = jnp.bfloat16

D_MODEL = 1024
DEPTH = 4
N_MIXERS = 3
EPS = 1e-6

GM_CHUNK = 128
GM_GROUPS = 8
GM_INNER = 2 * D_MODEL
GM_GROUP_DIM = GM_INNER // GM_GROUPS

MLA_HEADS = 8
MLA_Q_RANK = D_MODEL // 4
MLA_KV_RANK = D_MODEL // 8
MLA_NOPE = 128
MLA_ROPE = 64
MLA_V = 128
ROPE_THETA = 10000.0

HG_HEADS = 8
HG_DK = 128
HG_DV = D_MODEL // HG_HEADS

FF_DIM = 2816
CONV_W = 3

LANES = 128
CONV_TAIL = 8
VMEM_LIMIT = 56 * 1024 * 1024

TM_FFN = 512
FC_FFN = 256
TM_GMLP = 512
TM_PROJ = 512
TQ_ATTN = 512
FLASH_HEADS = 4
TB_HGRN = 512
HG_KCHUNK = 128
HG_LEVELS = HG_KCHUNK.bit_length() - 1
LOG2E = 1.4426950408889634
MLA_HEAD_PAD = MLA_NOPE + LANES


def _params(sem):
    return pltpu.CompilerParams(dimension_semantics=sem, vmem_limit_bytes=VMEM_LIMIT)


def _resident(arr):
    zeros = (0,) * arr.ndim
    return pl.BlockSpec(arr.shape, lambda *_: zeros, pipeline_mode=pl.Buffered(1))


def _rms(x, g):
    return x * lax.rsqrt(jnp.mean(x * x, axis=-1, keepdims=True) + EPS) * g


def _rms_mod(x, g, shift, scale):
    return _rms(x, g) * (1.0 + scale) + shift


def _dot(a, b):
    return jnp.dot(a, b, preferred_element_type=F32)


def _dot_nt(a, b):
    return lax.dot_general(a, b, (((1,), (1,)), ((), ())), preferred_element_type=F32)


def _dot_tn(a, b):
    return lax.dot_general(a, b, (((0,), (0,)), ((), ())), preferred_element_type=F32)


def _silu(x):
    return x * jax.nn.sigmoid(x)


def _adaln_kernel(c_ref, w_ref, b_ref, o_ref):
    c_act = _silu(c_ref[...]).astype(BF16)
    o_ref[0] = _dot(c_act, w_ref[0].astype(BF16)) + b_ref[0]


def _adaln(c, ada_w, ada_b):
    batch = c.shape[0]
    rows = 8
    tn = 1536
    c_pad = jnp.pad(c, ((0, rows - batch), (0, 0)))
    n = 6 * D_MODEL
    mod = pl.pallas_call(
        _adaln_kernel,
        out_shape=jax.ShapeDtypeStruct((DEPTH, rows, n), F32),
        grid=(DEPTH, n // tn),
        in_specs=[
            pl.BlockSpec((rows, D_MODEL), lambda i, j: (0, 0)),
            pl.BlockSpec((1, D_MODEL, tn), lambda i, j: (i, 0, j)),
            pl.BlockSpec((1, 1, tn), lambda i, j: (i, 0, j)),
        ],
        out_specs=pl.BlockSpec((1, rows, tn), lambda i, j: (i, 0, j)),
        compiler_params=_params(("parallel", "parallel")),
        name="adaln_mod",
    )(c_pad, ada_w, ada_b.reshape(DEPTH, 1, n))
    return mod[:, :batch].reshape(DEPTH, batch, 6, D_MODEL)


def _ffn_kernel(*refs, final_norm, fused_proj):
    if fused_proj:
        a_ref, wo_ref, *refs = refs
    x_ref, mod_ref, ng_ref, wu_ref, cw_ref, cb_ref, wd_ref, fg_ref, o_ref, h_scr, act_scr, tail_scr, x_scr = refs
    mod = mod_ref[0]
    tm = x_ref.shape[1]
    tail = tail_scr.shape[1]

    x = x_ref[0]
    if fused_proj:
        if len(a_ref.shape) == 4:
            a = jnp.concatenate([a_ref[0, hd] for hd in range(a_ref.shape[1])], axis=-1)
        else:
            a = a_ref[0]
        x = x + (1.0 + mod[2:3]) * _dot(a, wo_ref[...])
        x_scr[...] = x
    h_scr[...] = _rms_mod(x, ng_ref[...], mod[3:4], mod[4:5]).astype(BF16)
    h = h_scr[...]

    @pl.when(pl.program_id(1) == 0)
    def _():
        tail_scr[...] = jnp.zeros_like(tail_scr)

    def conv(ci):
        lo = ci * FC_FFN
        a = _dot(h, wu_ref[:, lo:lo + FC_FFN])
        ext = jnp.concatenate([tail_scr[ci], a], axis=0)
        tail_scr[ci] = a[tm - tail:]
        cw = cw_ref[:, lo:lo + FC_FFN]
        return (cb_ref[:, lo:lo + FC_FFN] + cw[2:3] * a + cw[1:2] * pltpu.roll(ext, 1, 0)[tail:]
                + cw[0:1] * pltpu.roll(ext, 2, 0)[tail:])

    n_chunks = FF_DIM // FC_FFN
    for c in range(n_chunks):
        act_scr[:, c * FC_FFN:(c + 1) * FC_FFN] = (_silu(conv(c)) * conv(n_chunks + c)).astype(BF16)

    x = x_scr[...] if fused_proj else x_ref[0]
    out = x + (1.0 + mod[5:6]) * _dot(act_scr[...], wd_ref[...])
    if final_norm:
        out = _rms(out, fg_ref[...])
    o_ref[0] = out


def _conv_ffn(x, mod, norm_g, w_up, conv_w, conv_b, w_down, final_g, final_norm, proj=None):
    batch, seq, d = x.shape
    tm = TM_FFN
    const = lambda b, s: (0, 0)
    conv_b = conv_b.reshape(1, -1)
    tok = pl.BlockSpec((1, tm, d), lambda b, s: (b, s, 0))
    operands, in_specs = [], []
    if proj is not None:
        a, w_o = proj
        if a.ndim == 4:
            in_specs.append(pl.BlockSpec((1, a.shape[1], tm, a.shape[3]), lambda b, s: (b, 0, s, 0)))
        else:
            in_specs.append(pl.BlockSpec((1, tm, a.shape[2]), lambda b, s: (b, s, 0)))
        in_specs.append(_resident(w_o))
        operands += [a, w_o]
    operands += [x, mod, norm_g.reshape(1, d), w_up, conv_w, conv_b, w_down, final_g.reshape(1, d)]
    in_specs += [tok, pl.BlockSpec((1, 6, d), lambda b, s: (b, 0, 0)), pl.BlockSpec((1, d), const), _resident(w_up),
                 _resident(conv_w), _resident(conv_b), _resident(w_down), pl.BlockSpec((1, d), const)]
    return pl.pallas_call(
        functools.partial(_ffn_kernel, final_norm=final_norm, fused_proj=proj is not None),
        out_shape=jax.ShapeDtypeStruct(x.shape, F32),
        grid=(batch, seq // tm),
        in_specs=in_specs,
        out_specs=tok,
        scratch_shapes=[pltpu.VMEM((tm, d), BF16), pltpu.VMEM((tm, FF_DIM), BF16),
                        pltpu.VMEM((2 * FF_DIM // FC_FFN, CONV_TAIL, FC_FFN), F32),
                        pltpu.VMEM((tm, d) if proj is not None else (8, LANES), F32)],
        compiler_params=_params(("parallel", "arbitrary")),
        name="conv_ffn",
    )(*operands)


def _gmlp_kernel(x_ref, mod_ref, ng_ref, win_ref, lng_ref, lnb_ref, ws_ref, bs_ref, wout_ref, o_ref, v_scr,
                 y_scr):
    mod = mod_ref[0]
    x = x_ref[0]
    tm = x.shape[0]
    h = _rms_mod(x, ng_ref[...], mod[0:1], mod[1:2]).astype(BF16)

    v = jax.nn.gelu(_dot(h, win_ref[:, GM_INNER:]))
    mu = jnp.mean(v, axis=-1, keepdims=True)
    vc = v - mu
    var = jnp.mean(vc * vc, axis=-1, keepdims=True)
    v_scr[...] = (vc * lax.rsqrt(var + EPS) * lng_ref[...] + lnb_ref[...]).astype(BF16)

    row = lax.broadcasted_iota(jnp.int32, (GM_CHUNK, GM_CHUNK), 0)
    col = lax.broadcasted_iota(jnp.int32, (GM_CHUNK, GM_CHUNK), 1)
    causal = row >= col

    for g in range(GM_GROUPS):
        lo, hi = g * GM_GROUP_DIM, (g + 1) * GM_GROUP_DIM
        u = jax.nn.gelu(_dot(h, win_ref[:, lo:hi]))
        w_s = jnp.where(causal, ws_ref[g], 0.0).astype(BF16)
        bias = bs_ref[:, lo:hi]
        for c in range(tm // GM_CHUNK):
            rows = slice(c * GM_CHUNK, (c + 1) * GM_CHUNK)
            sv = _dot(w_s, v_scr[rows, lo:hi]) + bias
            y_scr[rows, lo:hi] = (u[rows] * sv).astype(BF16)
    o_ref[0] = x + (1.0 + mod[2:3]) * _dot(y_scr[...], wout_ref[...])


def _gmlp_mixer(x, mod, norm_g, w_in, ln_g, ln_b, w_s, b_s, w_out):
    batch, seq, d = x.shape
    tm = TM_GMLP
    bias = jnp.repeat(b_s.T, GM_GROUP_DIM, axis=1)
    const = lambda b, s: (0, 0)
    return pl.pallas_call(
        _gmlp_kernel,
        out_shape=jax.ShapeDtypeStruct(x.shape, F32),
        grid=(batch, seq // tm),
        in_specs=[
            pl.BlockSpec((1, tm, d), lambda b, s: (b, s, 0)),
            pl.BlockSpec((1, 6, d), lambda b, s: (b, 0, 0)),
            pl.BlockSpec((1, d), const),
            _resident(w_in),
            pl.BlockSpec((1, GM_INNER), const),
            pl.BlockSpec((1, GM_INNER), const),
            _resident(w_s),
            _resident(bias),
            _resident(w_out),
        ],
        out_specs=pl.BlockSpec((1, tm, d), lambda b, s: (b, s, 0)),
        scratch_shapes=[pltpu.VMEM((tm, GM_INNER), BF16), pltpu.VMEM((tm, GM_INNER), BF16)],
        compiler_params=_params(("parallel", "parallel")),
        name="gmlp_mixer",
    )(x, mod, norm_g.reshape(1, d), w_in, ln_g.reshape(1, -1), ln_b.reshape(1, -1), w_s, bias, w_out)


def _mla_proj_kernel(x_ref, pos_ref, mod_ref, ng_ref, invf_ref, wa_ref, qg_ref, kvg_ref, wq_ref, wqs_ref,
                     wk_ref, wv_ref, q_ref, k_ref, v_ref, *, scale):
    mod = mod_ref[0]
    h = _rms_mod(x_ref[0], ng_ref[...], mod[0:1], mod[1:2]).astype(BF16)
    a = _dot(h, wa_ref[...])
    cq = _rms(a[:, :MLA_Q_RANK], qg_ref[...]).astype(BF16)
    kv_lo = MLA_Q_RANK
    ckv = _rms(a[:, kv_lo:kv_lo + MLA_KV_RANK], kvg_ref[...]).astype(BF16)
    kr_lo = kv_lo + MLA_KV_RANK
    k_rope, k_rope_sw = a[:, kr_lo:kr_lo + LANES], a[:, kr_lo + LANES:kr_lo + 2 * LANES]

    ang = pos_ref[0] * invf_ref[...]
    cos, sin = jnp.cos(ang), jnp.sin(ang)

    q_main = _dot(cq, wq_ref[...])
    q_sw = _dot(cq, wqs_ref[...])
    k_nope = _dot(ckv, wk_ref[...])
    v_ref[0] = _dot(ckv, wv_ref[...]).astype(BF16)
    k_r = (k_rope * cos + k_rope_sw * sin).astype(BF16)
    for hd in range(MLA_HEADS):
        lo = hd * MLA_HEAD_PAD
        mid = lo + MLA_NOPE
        q_ref[0, :, lo:mid] = (q_main[:, lo:mid] * scale).astype(BF16)
        q_r = q_main[:, mid:mid + LANES] * cos + q_sw[:, hd * LANES:(hd + 1) * LANES] * sin
        q_ref[0, :, mid:mid + LANES] = (q_r * scale).astype(BF16)
        k_ref[0, :, lo:mid] = k_nope[:, hd * MLA_NOPE:(hd + 1) * MLA_NOPE].astype(BF16)
        k_ref[0, :, mid:mid + LANES] = k_r


def _flash_kernel(q_ref, k_ref, v_ref, o_ref, m_scr, l_scr, acc_scr):
    qi = pl.program_id(2)
    t = q_ref.shape[1]

    m_scr[...] = jnp.full_like(m_scr, -jnp.inf)
    l_scr[...] = jnp.zeros_like(l_scr)
    acc_scr[...] = jnp.zeros_like(acc_scr)

    def block(j, masked):
        rows = pl.ds(pl.multiple_of(j * t, t), t)
        for hd in range(FLASH_HEADS):
            qk_lanes = slice(hd * MLA_HEAD_PAD, (hd + 1) * MLA_HEAD_PAD)
            v_lanes = slice(hd * MLA_V, (hd + 1) * MLA_V)
            s = _dot_nt(q_ref[0, :, qk_lanes], k_ref[0, rows, qk_lanes])
            if masked:
                row = lax.broadcasted_iota(jnp.int32, (t, t), 0)
                col = lax.broadcasted_iota(jnp.int32, (t, t), 1)
                s = jnp.where(row >= col, s, -jnp.inf)
            m_prev = m_scr[hd]
            m_next = jnp.maximum(m_prev, jnp.max(s, axis=1, keepdims=True))
            p = jnp.exp2(s - jnp.concatenate([m_next] * (t // LANES), axis=1))
            alpha = jnp.exp2(m_prev - m_next)
            l_scr[hd] = alpha * l_scr[hd] + jnp.sum(p, axis=1, keepdims=True)
            acc_scr[hd] = alpha * acc_scr[hd] + _dot(p.astype(BF16), v_ref[0, rows, v_lanes])
            m_scr[hd] = m_next

    def unmasked(j, carry):
        block(j, False)
        return carry

    lax.fori_loop(0, qi, unmasked, 0)
    block(qi, True)
    for hd in range(FLASH_HEADS):
        o_ref[0, :, hd * MLA_V:(hd + 1) * MLA_V] = (acc_scr[hd] / l_scr[hd]).astype(BF16)


def _mla_mixer(x, positions, mod, norm_g, w_a, q_norm_g, kv_norm_g, w_qb, w_kvb, w_o):
    batch, seq, d = x.shape
    heads, rope, half = MLA_HEADS, MLA_ROPE, MLA_ROPE // 2
    pad = LANES - rope

    def swap_halves(w):
        return jnp.concatenate([-w[..., half:], w[..., :half]], axis=-1)

    def pad_lanes(w):
        return jnp.pad(w, [(0, 0)] * (w.ndim - 1) + [(0, pad)])

    kr_lo = MLA_Q_RANK + MLA_KV_RANK
    w_kr = w_a[:, kr_lo:]
    w_a_ext = jnp.concatenate([w_a[:, :kr_lo], pad_lanes(w_kr), pad_lanes(swap_halves(w_kr))], axis=1)

    wq = w_qb.reshape(MLA_Q_RANK, heads, MLA_NOPE + rope)
    wq_rope = wq[:, :, MLA_NOPE:]
    w_q_main = jnp.concatenate([wq[:, :, :MLA_NOPE], pad_lanes(wq_rope)], axis=-1).reshape(MLA_Q_RANK, -1)
    w_q_sw = pad_lanes(swap_halves(wq_rope)).reshape(MLA_Q_RANK, -1)

    wkv = w_kvb.reshape(MLA_KV_RANK, heads, MLA_NOPE + MLA_V)
    w_k = wkv[:, :, :MLA_NOPE].reshape(MLA_KV_RANK, -1)
    w_v = wkv[:, :, MLA_NOPE:].reshape(MLA_KV_RANK, -1)

    inv_freq = ROPE_THETA ** (-jnp.arange(0, rope, 2, dtype=F32) / rope)
    inv_freq = jnp.tile(inv_freq, LANES // half).reshape(1, LANES)
    pos = positions.astype(F32).reshape(batch, seq, 1)

    tm = TM_PROJ
    qk_w = heads * MLA_HEAD_PAD
    const = lambda b, s: (0, 0)
    full = lambda arr: pl.BlockSpec(arr.shape, const)
    weights = [w_a_ext.astype(BF16), q_norm_g.reshape(1, -1), kv_norm_g.reshape(1, -1), w_q_main.astype(BF16),
               w_q_sw.astype(BF16), w_k.astype(BF16), w_v.astype(BF16)]
    q, k, v = pl.pallas_call(
        functools.partial(_mla_proj_kernel, scale=LOG2E * (MLA_NOPE + rope) ** -0.5),
        out_shape=(jax.ShapeDtypeStruct((batch, seq, qk_w), BF16),
                   jax.ShapeDtypeStruct((batch, seq, qk_w), BF16),
                   jax.ShapeDtypeStruct((batch, seq, heads * MLA_V), BF16)),
        grid=(batch, seq // tm),
        in_specs=[
            pl.BlockSpec((1, tm, d), lambda b, s: (b, s, 0)),
            pl.BlockSpec((1, tm, 1), lambda b, s: (b, s, 0)),
            pl.BlockSpec((1, 6, d), lambda b, s: (b, 0, 0)),
            pl.BlockSpec((1, d), const),
            pl.BlockSpec((1, LANES), const),
        ] + [full(w) for w in weights],
        out_specs=(pl.BlockSpec((1, tm, qk_w), lambda b, s: (b, s, 0)),
                   pl.BlockSpec((1, tm, qk_w), lambda b, s: (b, s, 0)),
                   pl.BlockSpec((1, tm, heads * MLA_V), lambda b, s: (b, s, 0))),
        compiler_params=_params(("parallel", "parallel")),
        name="mla_proj",
    )(x, pos, mod, norm_g.reshape(1, d), inv_freq, *weights)

    t = TQ_ATTN
    fh = FLASH_HEADS
    o = pl.pallas_call(
        _flash_kernel,
        out_shape=jax.ShapeDtypeStruct((batch, seq, heads * MLA_V), BF16),
        grid=(batch, heads // fh, seq // t),
        in_specs=[
            pl.BlockSpec((1, t, fh * MLA_HEAD_PAD), lambda b, h, i: (b, i, h)),
            pl.BlockSpec((1, seq, fh * MLA_HEAD_PAD), lambda b, h, i: (b, 0, h)),
            pl.BlockSpec((1, seq, fh * MLA_V), lambda b, h, i: (b, 0, h)),
        ],
        out_specs=pl.BlockSpec((1, t, fh * MLA_V), lambda b, h, i: (b, i, h)),
        scratch_shapes=[pltpu.VMEM((fh, t, LANES), F32), pltpu.VMEM((fh, t, LANES), F32),
                        pltpu.VMEM((fh, t, MLA_V), F32)],
        compiler_params=_params(("parallel", "parallel", "arbitrary")),
        name="mla_flash",
    )(q, k, v)
    return o, w_o.astype(BF16)


def _hgrn_proj_kernel(x_ref, mod_ref, ng_ref, lbp_ref, w_ref, q_ref, k_ref, lf_ref, v_ref, sg_ref, *, layer):
    mod = mod_ref[0]
    h = _rms_mod(x_ref[0], ng_ref[...], mod[0:1], mod[1:2]).astype(BF16)
    nk = HG_HEADS * HG_DK
    nv = HG_HEADS * HG_DV

    lbp = lbp_ref[...]
    e = jnp.exp(lbp - jnp.max(lbp, axis=0, keepdims=True))
    p = e / jnp.sum(e, axis=0, keepdims=True)
    lb = jnp.sum(p[1:layer + 1], axis=0, keepdims=True) if layer >= 1 else jnp.zeros((1, nk), F32)

    def store_heads(ref, val, width):
        for hd in range(HG_HEADS):
            ref[0, hd] = val[:, hd * width:(hd + 1) * width]

    store_heads(q_ref, _dot(h, w_ref[:, :nk]).astype(BF16), HG_DK)
    f = lb + (1.0 - lb) * jax.nn.sigmoid(_dot(h, w_ref[:, nk:2 * nk]))
    store_heads(lf_ref, jnp.log(f), HG_DK)
    store_heads(k_ref, (1.0 - f).astype(BF16), HG_DK)
    store_heads(v_ref, _dot(h, w_ref[:, 2 * nk:2 * nk + nv]).astype(BF16), HG_DV)
    store_heads(sg_ref, _silu(_dot(h, w_ref[:, 2 * nk + nv:])).astype(BF16), HG_DV)


def _hgrn_fill_tables(lt_scr, sgn_scr, bm_scr):
    c = HG_KCHUNK
    row = lax.broadcasted_iota(jnp.int32, (c, HG_DK), 0)
    trow = lax.broadcasted_iota(jnp.int32, (c, c), 0)
    tcol = lax.broadcasted_iota(jnp.int32, (c, c), 1)
    lower = jnp.where(trow >= tcol, 1.0, 0.0).astype(BF16)
    for part in range(3):
        lt_scr[:, part * c:(part + 1) * c] = lower
    bm_scr[0] = jnp.where(trow == tcol, 1.0, 0.0)
    for li in range(HG_LEVELS):
        half = 1 << li
        sgn_scr[li] = jnp.where((row & half) != 0, LOG2E, 0.0 if li == 0 else -LOG2E)
        owned = ((trow & -(2 * half)) == (tcol & -(2 * half))) & ((trow & half) != 0) & ((tcol & half) == 0)
        bm_scr[li + 1] = jnp.where(owned, 1.0, 0.0)


def _hgrn_chunk(q, kk, g, v, state_t, b_scr, lt_scr, sgn_scr, bm_scr):
    c_rows = q.shape[0]

    g_hi = g.astype(BF16)
    rest = g - g_hi.astype(F32)
    g_mid = rest.astype(BF16)
    g_lo = (rest - g_mid.astype(F32)).astype(BF16)
    b = _dot(lt_scr[...], jnp.concatenate([g_hi, g_mid, g_lo], axis=0))
    b_scr[...] = b

    terms = [(q.astype(BF16), kk.astype(BF16))]
    for li in range(HG_LEVELS):
        half = 1 << li
        blk = 2 * half
        if li == 0:
            arg = g * sgn_scr[0]
        else:
            if blk < 8:
                low_rows = jnp.where(lax.broadcasted_iota(jnp.int32, (8, HG_DK), 0) < blk, 1.0, 0.0)
                pieces = [low_rows * b_scr[r + half - 1:r + half, :]
                          + (1.0 - low_rows) * b_scr[r + blk + half - 1:r + blk + half, :]
                          for r in range(0, c_rows, 8)]
            else:
                pieces = [jnp.broadcast_to(b_scr[r + half - 1:r + half, :], (blk, HG_DK))
                          for r in range(0, c_rows, blk)]
            arg = (b - jnp.concatenate(pieces, axis=0)) * sgn_scr[li]
        x = jnp.exp2(arg)
        terms.append(((q * x).astype(BF16), (kk * x).astype(BF16)))

    attn = sum(_dot_nt(q_i, k_i) * bm_scr[i] for i, (q_i, k_i) in enumerate(terms))

    b_end = b_scr[c_rows - 1:c_rows, :]
    o = _dot(attn.astype(BF16), v)
    o = o + _dot_nt((q * jnp.exp(b)).astype(BF16), state_t.astype(BF16))
    k_dec = (kk * jnp.exp(b_end - b)).astype(BF16)
    new_state_t = jnp.exp(b_end) * state_t + _dot_tn(v, k_dec)
    return o, new_state_t


def _hgrn_rec_kernel(q_ref, k_ref, lf_ref, v_ref, sg_ref, g_ref, o_ref, state_scr, b_scr, lt_scr, sgn_scr, bm_scr):
    @pl.when(pl.program_id(1) == 0)
    def _():
        state_scr[...] = jnp.zeros_like(state_scr)
        _hgrn_fill_tables(lt_scr, sgn_scr, bm_scr)

    norm_g = g_ref[...]
    tb = q_ref.shape[2]

    def chunk(c, carry):
        rows = pl.ds(pl.multiple_of(c * HG_KCHUNK, HG_KCHUNK), HG_KCHUNK)
        for hd in range(HG_HEADS):
            o, state_scr[hd] = _hgrn_chunk(
                q_ref[0, hd, rows, :].astype(F32), k_ref[0, hd, rows, :].astype(F32), lf_ref[0, hd, rows, :],
                v_ref[0, hd, rows, :], state_scr[hd], b_scr.at[hd], lt_scr, sgn_scr, bm_scr)
            o_ref[0, hd, rows, :] = (_rms(o, norm_g) * sg_ref[0, hd, rows, :].astype(F32)).astype(BF16)
        return carry

    lax.fori_loop(0, tb // HG_KCHUNK, chunk, 0)


def _hgrn_mixer(x, mod, norm_g, hg_lb, layer, w_in, out_norm_g, w_o):
    batch, seq, d = x.shape
    nk = HG_HEADS * HG_DK
    nv = HG_HEADS * HG_DV
    tm = TM_PROJ
    const = lambda b, s: (0, 0)
    heads_shape = lambda w, dt: jax.ShapeDtypeStruct((batch, HG_HEADS, seq, w), dt)
    heads_blk = lambda rows, w: pl.BlockSpec((1, HG_HEADS, rows, w), lambda b, s: (b, 0, s, 0))
    q, k, lf, v, sg = pl.pallas_call(
        functools.partial(_hgrn_proj_kernel, layer=layer),
        out_shape=(heads_shape(HG_DK, BF16), heads_shape(HG_DK, BF16), heads_shape(HG_DK, F32),
                   heads_shape(HG_DV, BF16), heads_shape(HG_DV, BF16)),
        grid=(batch, seq // tm),
        in_specs=[
            pl.BlockSpec((1, tm, d), lambda b, s: (b, s, 0)),
            pl.BlockSpec((1, 6, d), lambda b, s: (b, 0, 0)),
            pl.BlockSpec((1, d), const),
            pl.BlockSpec((DEPTH, nk), const),
            pl.BlockSpec(w_in.shape, const),
        ],
        out_specs=(heads_blk(tm, HG_DK), heads_blk(tm, HG_DK), heads_blk(tm, HG_DK), heads_blk(tm, HG_DV),
                   heads_blk(tm, HG_DV)),
        compiler_params=_params(("parallel", "parallel")),
        name="hgrn_proj",
    )(x, mod, norm_g.reshape(1, d), hg_lb, w_in)

    tb = TB_HGRN
    o = pl.pallas_call(
        _hgrn_rec_kernel,
        out_shape=heads_shape(HG_DV, BF16),
        grid=(batch, seq // tb),
        in_specs=[heads_blk(tb, HG_DK), heads_blk(tb, HG_DK), heads_blk(tb, HG_DK), heads_blk(tb, HG_DV),
                  heads_blk(tb, HG_DV), pl.BlockSpec((1, HG_DV), const)],
        out_specs=heads_blk(tb, HG_DV),
        scratch_shapes=[pltpu.VMEM((HG_HEADS, HG_DV, HG_DK), F32), pltpu.VMEM((HG_HEADS, HG_KCHUNK, HG_DK), F32),
                        pltpu.VMEM((HG_KCHUNK, 3 * HG_KCHUNK), BF16),
                        pltpu.VMEM((HG_LEVELS, HG_KCHUNK, HG_DK), F32),
                        pltpu.VMEM((HG_LEVELS + 1, HG_KCHUNK, HG_KCHUNK), F32)],
        compiler_params=_params(("parallel", "arbitrary")),
        name="hgrn_recurrence",
    )(q, k, lf, v, sg, out_norm_g.reshape(1, HG_DV))
    return o, w_o


def kernel(x, c, positions, ada_w, ada_b, mix_norm_g, ffn_norm_g, gm_w_in, gm_ln_g, gm_ln_b, gm_w_s, gm_b_s,
           gm_w_out, mla_w_a, mla_q_norm_g, mla_kv_norm_g, mla_w_qb, mla_w_kvb, mla_w_o, hg_lb, hg_w_in,
           hg_norm_g, hg_w_o, ff_w_up, ff_conv_w, ff_conv_b, ff_w_down, final_g):
    mod_all = _adaln(c, ada_w, ada_b)
    for i in range(DEPTH):
        mod = mod_all[i]
        kind, j = i % N_MIXERS, i // N_MIXERS
        proj = None
        if kind == 0:
            x = _gmlp_mixer(x, mod, mix_norm_g[i], gm_w_in[j].astype(BF16), gm_ln_g[j], gm_ln_b[j], gm_w_s[j],
                            gm_b_s[j], gm_w_out[j].astype(BF16))
        elif kind == 1:
            proj = _mla_mixer(x, positions, mod, mix_norm_g[i], mla_w_a[j], mla_q_norm_g[j], mla_kv_norm_g[j],
                              mla_w_qb[j], mla_w_kvb[j], mla_w_o[j])
        else:
            proj = _hgrn_mixer(x, mod, mix_norm_g[i], hg_lb, i, hg_w_in[j].astype(BF16), hg_norm_g[j],
                               hg_w_o[j].astype(BF16))
        x = _conv_ffn(x, mod, ffn_norm_g[i], ff_w_up[i].astype(BF16), ff_conv_w[i], ff_conv_b[i],
                      ff_w_down[i].astype(BF16), final_g, final_norm=(i == DEPTH - 1), proj=proj)
    return x
```

```python
import functools

import jax
import jax.numpy as jnp
from jax import lax
from jax.experimental import pallas as pl
from jax.experimental.pallas import tpu as pltpu

F32 = jnp.float32
BF16 = jnp.bfloat16

D_MODEL = 1024
DEPTH = 4
N_MIXERS = 3
EPS = 1e-6

GM_CHUNK = 128
GM_GROUPS = 8
GM_INNER = 2 * D_MODEL
GM_GROUP_DIM = GM_INNER // GM_GROUPS

MLA_HEADS = 8
MLA_Q_RANK = D_MODEL // 4
MLA_KV_RANK = D_MODEL // 8
MLA_NOPE = 128
MLA_ROPE = 64
MLA_V = 128
ROPE_THETA = 10000.0

HG_HEADS = 8
HG_DK = 128
HG_DV = D_MODEL // HG_HEADS

FF_DIM = 2816
CONV_W = 3

LANES = 128
CONV_TAIL = 8
VMEM_LIMIT = 56 * 1024 * 1024

TM_FFN = 512
FC_FFN = 256
TM_GMLP = 512
TM_PROJ = 512
TQ_ATTN = 512
FLASH_HEADS = 4
TB_HGRN = 256
HG_KCHUNK = 128
LOG2E = 1.4426950408889634
MLA_HEAD_PAD = MLA_NOPE + LANES


def _params(sem):
    return pltpu.CompilerParams(dimension_semantics=sem, vmem_limit_bytes=VMEM_LIMIT)


def _resident(arr, layer=None):
    if layer is None:
        shape, index = arr.shape, (0,) * arr.ndim
    else:
        shape, index = (None,) + arr.shape[1:], (layer,) + (0,) * (arr.ndim - 1)
    return pl.BlockSpec(shape, lambda *_: index, pipeline_mode=pl.Buffered(1))


def _rms(x, g):
    return x * lax.rsqrt(jnp.mean(x * x, axis=-1, keepdims=True) + EPS) * g


def _rms_mod(x, g, shift, scale):
    return _rms(x, g) * (1.0 + scale) + shift


def _dot(a, b):
    return jnp.dot(a, b, preferred_element_type=F32)


def _dot_nt(a, b):
    return lax.dot_general(a, b, (((1,), (1,)), ((), ())), preferred_element_type=F32)


def _dot_tn(a, b):
    return lax.dot_general(a, b, (((0,), (0,)), ((), ())), preferred_element_type=F32)


def _silu(x):
    return x / (1.0 + jnp.exp2(x * (-LOG2E)))


def _gelu_tanh(x):
    c1 = -2.0 * LOG2E * 0.7978845608028654
    return x / (1.0 + jnp.exp2(x * (c1 + (c1 * 0.044715) * (x * x))))


def _adaln_kernel(c_ref, w_ref, b_ref, o_ref):
    c_act = _silu(c_ref[...]).astype(BF16)
    o_ref[0] = _dot(c_act, w_ref[0].astype(BF16)) + b_ref[0]


def _adaln(c, ada_w, ada_b):
    batch = c.shape[0]
    rows = 8
    tn = 1536
    c_pad = jnp.pad(c, ((0, rows - batch), (0, 0)))
    n = 6 * D_MODEL
    mod = pl.pallas_call(
        _adaln_kernel,
        out_shape=jax.ShapeDtypeStruct((DEPTH, rows, n), F32),
        grid=(DEPTH, n // tn),
        in_specs=[
            pl.BlockSpec((rows, D_MODEL), lambda i, j: (0, 0)),
            pl.BlockSpec((1, D_MODEL, tn), lambda i, j: (i, 0, j)),
            pl.BlockSpec((1, 1, tn), lambda i, j: (i, 0, j)),
        ],
        out_specs=pl.BlockSpec((1, rows, tn), lambda i, j: (i, 0, j)),
        compiler_params=_params(("parallel", "parallel")),
        name="adaln_mod",
    )(c_pad, ada_w, ada_b.reshape(DEPTH, 1, n))
    return mod[:, :batch].reshape(DEPTH, batch, 6, D_MODEL)


def _ffn_kernel(*refs, final_norm, fused_proj):
    if fused_proj:
        a_ref, wo_ref, *refs = refs
    x_ref, mod_ref, ng_ref, wu_ref, cw_ref, cb_ref, wd_ref, fg_ref, o_ref, h_scr, act_scr, tail_scr, x_scr = refs
    mod = mod_ref[0]
    tm = x_ref.shape[1]
    tail = tail_scr.shape[1]
    n_chunks = FF_DIM // FC_FFN

    x = x_ref[0]
    if fused_proj:
        if len(a_ref.shape) == 4:
            a = jnp.concatenate([a_ref[0, hd] for hd in range(a_ref.shape[1])], axis=-1)
        else:
            a = a_ref[0]
        x = x + (1.0 + mod[2:3]) * _dot(a, wo_ref[...])
        x_scr[...] = x
    h_scr[...] = _rms_mod(x, ng_ref[...], mod[3:4], mod[4:5]).astype(BF16)
    h = h_scr[...]

    @pl.when(pl.program_id(1) == 0)
    def _():
        tail_scr[...] = jnp.zeros_like(tail_scr)

    def conv(ci):
        lo = ci * FC_FFN
        a = _dot(h, wu_ref[:, lo:lo + FC_FFN])
        ext = jnp.concatenate([tail_scr[ci], a], axis=0)
        tail_scr[ci] = a[tm - tail:]
        cw = cw_ref[:, lo:lo + FC_FFN]
        return (cb_ref[:, lo:lo + FC_FFN] + cw[2:3] * a + cw[1:2] * pltpu.roll(ext, 1, 0)[tail:]
                + cw[0:1] * pltpu.roll(ext, 2, 0)[tail:])

    for c in range(n_chunks):
        act_scr[:, c * FC_FFN:(c + 1) * FC_FFN] = (_silu(conv(c)) * conv(n_chunks + c)).astype(BF16)

    x = x_scr[...] if fused_proj else x_ref[0]
    out = x + (1.0 + mod[5:6]) * _dot(act_scr[...], wd_ref[...])
    if final_norm:
        out = _rms(out, fg_ref[...])
    o_ref[0] = out


def _conv_ffn(x, mod, norm_g, layer, w_up, conv_w, conv_b, w_down, final_g, final_norm, proj=None):
    batch, seq, d = x.shape
    tm = TM_FFN
    const = lambda b, s: (0, 0)
    conv_b = conv_b.reshape(DEPTH, 1, -1)
    tok = pl.BlockSpec((1, tm, d), lambda b, s: (b, s, 0))
    operands, in_specs = [], []
    if proj is not None:
        a, w_o = proj
        if a.ndim == 4:
            in_specs.append(pl.BlockSpec((1, a.shape[1], tm, a.shape[3]), lambda b, s: (b, 0, s, 0)))
        else:
            in_specs.append(pl.BlockSpec((1, tm, a.shape[2]), lambda b, s: (b, s, 0)))
        in_specs.append(_resident(w_o))
        operands += [a, w_o]
    operands += [x, mod, norm_g.reshape(1, d), w_up, conv_w, conv_b, w_down, final_g.reshape(1, d)]
    in_specs += [tok, pl.BlockSpec((1, 6, d), lambda b, s: (b, 0, 0)), pl.BlockSpec((1, d), const),
                 _resident(w_up, layer), _resident(conv_w, layer), _resident(conv_b, layer),
                 _resident(w_down, layer), pl.BlockSpec((1, d), const)]
    return pl.pallas_call(
        functools.partial(_ffn_kernel, final_norm=final_norm, fused_proj=proj is not None),
        out_shape=jax.ShapeDtypeStruct(x.shape, F32),
        grid=(batch, seq // tm),
        in_specs=in_specs,
        out_specs=tok,
        scratch_shapes=[pltpu.VMEM((tm, d), BF16), pltpu.VMEM((tm, FF_DIM), BF16),
                        pltpu.VMEM((2 * FF_DIM // FC_FFN, CONV_TAIL, FC_FFN), F32),
                        pltpu.VMEM((tm, d) if proj is not None else (8, LANES), F32)],
        compiler_params=_params(("parallel", "arbitrary")),
        name="conv_ffn",
    )(*operands)


def _gmlp_kernel(x_ref, mod_ref, ng_ref, win_ref, lng_ref, lnb_ref, ws_ref, bs_ref, wout_ref, o_ref, v_scr,
                 y_scr):
    mod = mod_ref[0]
    x = x_ref[0]
    tm = x.shape[0]
    h = _rms_mod(x, ng_ref[...], mod[0:1], mod[1:2]).astype(BF16)

    v = _gelu_tanh(_dot(h, win_ref[:, GM_INNER:]))
    mu = jnp.mean(v, axis=-1, keepdims=True)
    vc = v - mu
    var = jnp.mean(vc * vc, axis=-1, keepdims=True)
    v_scr[...] = (vc * lax.rsqrt(var + EPS) * lng_ref[...] + lnb_ref[...]).astype(BF16)

    row = lax.broadcasted_iota(jnp.int32, (GM_CHUNK, GM_CHUNK), 0)
    col = lax.broadcasted_iota(jnp.int32, (GM_CHUNK, GM_CHUNK), 1)
    causal = row >= col

    for g in range(GM_GROUPS):
        lo, hi = g * GM_GROUP_DIM, (g + 1) * GM_GROUP_DIM
        u = _gelu_tanh(_dot(h, win_ref[:, lo:hi]))
        w_s = jnp.where(causal, ws_ref[g], 0.0).astype(BF16)
        bias = bs_ref[:, lo:hi]
        for c in range(tm // GM_CHUNK):
            rows = slice(c * GM_CHUNK, (c + 1) * GM_CHUNK)
            sv = _dot(w_s, v_scr[rows, lo:hi]) + bias
            y_scr[rows, lo:hi] = (u[rows] * sv).astype(BF16)
    o_ref[0] = x + (1.0 + mod[2:3]) * _dot(y_scr[...], wout_ref[...])


def _gmlp_mixer(x, mod, norm_g, layer, w_in, ln_g, ln_b, w_s, b_s, w_out):
    batch, seq, d = x.shape
    tm = TM_GMLP
    bias = jnp.repeat(b_s.T, GM_GROUP_DIM, axis=1)
    const = lambda b, s: (0, 0)
    return pl.pallas_call(
        _gmlp_kernel,
        out_shape=jax.ShapeDtypeStruct(x.shape, F32),
        grid=(batch, seq // tm),
        in_specs=[
            pl.BlockSpec((1, tm, d), lambda b, s: (b, s, 0)),
            pl.BlockSpec((1, 6, d), lambda b, s: (b, 0, 0)),
            pl.BlockSpec((1, d), const),
            _resident(w_in, layer),
            pl.BlockSpec((1, GM_INNER), const),
            pl.BlockSpec((1, GM_INNER), const),
            _resident(w_s, layer),
            _resident(bias),
            _resident(w_out, layer),
        ],
        out_specs=pl.BlockSpec((1, tm, d), lambda b, s: (b, s, 0)),
        scratch_shapes=[pltpu.VMEM((tm, GM_INNER), BF16), pltpu.VMEM((tm, GM_INNER), BF16)],
        compiler_params=_params(("parallel", "parallel")),
        name="gmlp_mixer",
    )(x, mod, norm_g.reshape(1, d), w_in, ln_g.reshape(1, -1), ln_b.reshape(1, -1), w_s, bias, w_out)


def _mla_proj_kernel(x_ref, pos_ref, mod_ref, ng_ref, invf_ref, wa_ref, qg_ref, kvg_ref, wq_ref, wqs_ref,
                     wk_ref, wv_ref, q_ref, k_ref, v_ref, *, scale):
    mod = mod_ref[0]
    h = _rms_mod(x_ref[0], ng_ref[...], mod[0:1], mod[1:2]).astype(BF16)
    a = _dot(h, wa_ref[...])
    cq = _rms(a[:, :MLA_Q_RANK], qg_ref[...]).astype(BF16)
    kv_lo = MLA_Q_RANK
    ckv = _rms(a[:, kv_lo:kv_lo + MLA_KV_RANK], kvg_ref[...]).astype(BF16)
    kr_lo = kv_lo + MLA_KV_RANK
    k_rope, k_rope_sw = a[:, kr_lo:kr_lo + LANES], a[:, kr_lo + LANES:kr_lo + 2 * LANES]

    ang = pos_ref[0] * invf_ref[...]
    cos, sin = jnp.cos(ang), jnp.sin(ang)

    q_main = _dot(cq, wq_ref[...])
    q_sw = _dot(cq, wqs_ref[...])
    k_nope = _dot(ckv, wk_ref[...])
    v_ref[0] = _dot(ckv, wv_ref[...]).astype(BF16)
    k_r = (k_rope * cos + k_rope_sw * sin).astype(BF16)
    for hd in range(MLA_HEADS):
        lo = hd * MLA_HEAD_PAD
        mid = lo + MLA_NOPE
        q_ref[0, :, lo:mid] = (q_main[:, lo:mid] * scale).astype(BF16)
        q_r = q_main[:, mid:mid + LANES] * cos + q_sw[:, hd * LANES:(hd + 1) * LANES] * sin
        q_ref[0, :, mid:mid + LANES] = (q_r * scale).astype(BF16)
        k_ref[0, :, lo:mid] = k_nope[:, hd * MLA_NOPE:(hd + 1) * MLA_NOPE].astype(BF16)
        k_ref[0, :, mid:mid + LANES] = k_r


def _flash_kernel(q_ref, k_ref, v_ref, o_ref, m_scr, l_scr, acc_scr):
    qi = pl.program_id(2)
    t = q_ref.shape[1]

    m_scr[...] = jnp.full_like(m_scr, -jnp.inf)
    l_scr[...] = jnp.zeros_like(l_scr)
    acc_scr[...] = jnp.zeros_like(acc_scr)

    def block(j, masked):
        rows = pl.ds(pl.multiple_of(j * t, t), t)
        for hd in range(FLASH_HEADS):
            qk_lanes = slice(hd * MLA_HEAD_PAD, (hd + 1) * MLA_HEAD_PAD)
            v_lanes = slice(hd * MLA_V, (hd + 1) * MLA_V)
            s = _dot_nt(q_ref[0, :, qk_lanes], k_ref[0, rows, qk_lanes])
            if masked:
                row = lax.broadcasted_iota(jnp.int32, (t, t), 0)
                col = lax.broadcasted_iota(jnp.int32, (t, t), 1)
                s = jnp.where(row >= col, s, -jnp.inf)
            m_prev = m_scr[hd]
            m_next = jnp.maximum(m_prev, jnp.max(s, axis=1, keepdims=True))
            p = jnp.exp2(s - jnp.concatenate([m_next] * (t // LANES), axis=1))
            alpha = jnp.exp2(m_prev - m_next)
            l_scr[hd] = alpha * l_scr[hd] + jnp.sum(p, axis=1, keepdims=True)
            acc_scr[hd] = alpha * acc_scr[hd] + _dot(p.astype(BF16), v_ref[0, rows, v_lanes])
            m_scr[hd] = m_next

    def unmasked(j, carry):
        block(j, False)
        return carry

    lax.fori_loop(0, qi, unmasked, 0)
    block(qi, True)
    for hd in range(FLASH_HEADS):
        o_ref[0, :, hd * MLA_V:(hd + 1) * MLA_V] = (acc_scr[hd] / l_scr[hd]).astype(BF16)


def _mla_mixer(x, positions, mod, norm_g, w_a, q_norm_g, kv_norm_g, w_qb, w_kvb, w_o):
    batch, seq, d = x.shape
    heads, rope, half = MLA_HEADS, MLA_ROPE, MLA_ROPE // 2
    pad = LANES - rope

    def swap_halves(w):
        return jnp.concatenate([-w[..., half:], w[..., :half]], axis=-1)

    def pad_lanes(w):
        return jnp.pad(w, [(0, 0)] * (w.ndim - 1) + [(0, pad)])

    kr_lo = MLA_Q_RANK + MLA_KV_RANK
    w_kr = w_a[:, kr_lo:]
    w_a_ext = jnp.concatenate([w_a[:, :kr_lo], pad_lanes(w_kr), pad_lanes(swap_halves(w_kr))], axis=1)

    wq = w_qb.reshape(MLA_Q_RANK, heads, MLA_NOPE + rope)
    wq_rope = wq[:, :, MLA_NOPE:]
    w_q_main = jnp.concatenate([wq[:, :, :MLA_NOPE], pad_lanes(wq_rope)], axis=-1).reshape(MLA_Q_RANK, -1)
    w_q_sw = pad_lanes(swap_halves(wq_rope)).reshape(MLA_Q_RANK, -1)

    wkv = w_kvb.reshape(MLA_KV_RANK, heads, MLA_NOPE + MLA_V)
    w_k = wkv[:, :, :MLA_NOPE].reshape(MLA_KV_RANK, -1)
    w_v = wkv[:, :, MLA_NOPE:].reshape(MLA_KV_RANK, -1)

    inv_freq = ROPE_THETA ** (-jnp.arange(0, rope, 2, dtype=F32) / rope)
    inv_freq = jnp.tile(inv_freq, LANES // half).reshape(1, LANES)
    pos = positions.astype(F32).reshape(batch, seq, 1)

    tm = TM_PROJ
    qk_w = heads * MLA_HEAD_PAD
    const = lambda b, s: (0, 0)
    full = lambda arr: pl.BlockSpec(arr.shape, const)
    weights = [w_a_ext.astype(BF16), q_norm_g.reshape(1, -1), kv_norm_g.reshape(1, -1), w_q_main.astype(BF16),
               w_q_sw.astype(BF16), w_k.astype(BF16), w_v.astype(BF16)]
    q, k, v = pl.pallas_call(
        functools.partial(_mla_proj_kernel, scale=LOG2E * (MLA_NOPE + rope) ** -0.5),
        out_shape=(jax.ShapeDtypeStruct((batch, seq, qk_w), BF16),
                   jax.ShapeDtypeStruct((batch, seq, qk_w), BF16),
                   jax.ShapeDtypeStruct((batch, seq, heads * MLA_V), BF16)),
        grid=(batch, seq // tm),
        in_specs=[
            pl.BlockSpec((1, tm, d), lambda b, s: (b, s, 0)),
            pl.BlockSpec((1, tm, 1), lambda b, s: (b, s, 0)),
            pl.BlockSpec((1, 6, d), lambda b, s: (b, 0, 0)),
            pl.BlockSpec((1, d), const),
            pl.BlockSpec((1, LANES), const),
        ] + [full(w) for w in weights],
        out_specs=(pl.BlockSpec((1, tm, qk_w), lambda b, s: (b, s, 0)),
                   pl.BlockSpec((1, tm, qk_w), lambda b, s: (b, s, 0)),
                   pl.BlockSpec((1, tm, heads * MLA_V), lambda b, s: (b, s, 0))),
        compiler_params=_params(("parallel", "parallel")),
        name="mla_proj",
    )(x, pos, mod, norm_g.reshape(1, d), inv_freq, *weights)

    t = TQ_ATTN
    fh = FLASH_HEADS
    o = pl.pallas_call(
        _flash_kernel,
        out_shape=jax.ShapeDtypeStruct((batch, seq, heads * MLA_V), BF16),
        grid=(batch, heads // fh, seq // t),
        in_specs=[
            pl.BlockSpec((1, t, fh * MLA_HEAD_PAD), lambda b, h, i: (b, i, h)),
            pl.BlockSpec((1, seq, fh * MLA_HEAD_PAD), lambda b, h, i: (b, 0, h)),
            pl.BlockSpec((1, seq, fh * MLA_V), lambda b, h, i: (b, 0, h)),
        ],
        out_specs=pl.BlockSpec((1, t, fh * MLA_V), lambda b, h, i: (b, i, h)),
        scratch_shapes=[pltpu.VMEM((fh, t, LANES), F32), pltpu.VMEM((fh, t, LANES), F32),
                        pltpu.VMEM((fh, t, MLA_V), F32)],
        compiler_params=_params(("parallel", "parallel", "arbitrary")),
        name="mla_flash",
    )(q, k, v)
    return o, w_o.astype(BF16)


def _hgrn_proj_kernel(x_ref, mod_ref, ng_ref, lbp_ref, w_ref, q_ref, k_ref, lf_ref, v_ref, sg_ref, *, layer):
    mod = mod_ref[0]
    h = _rms_mod(x_ref[0], ng_ref[...], mod[0:1], mod[1:2]).astype(BF16)
    nk = HG_HEADS * HG_DK
    nv = HG_HEADS * HG_DV

    lbp = lbp_ref[...]
    e = jnp.exp(lbp - jnp.max(lbp, axis=0, keepdims=True))
    p = e / jnp.sum(e, axis=0, keepdims=True)
    lb = jnp.sum(p[1:layer + 1], axis=0, keepdims=True) if layer >= 1 else jnp.zeros((1, nk), F32)

    def store_heads(ref, val, width):
        for hd in range(HG_HEADS):
            ref[0, hd] = val[:, hd * width:(hd + 1) * width]

    store_heads(q_ref, _dot(h, w_ref[:, :nk]).astype(BF16), HG_DK)
    f = lb + (1.0 - lb) * jax.nn.sigmoid(_dot(h, w_ref[:, nk:2 * nk]))
    store_heads(lf_ref, jnp.log(f), HG_DK)
    store_heads(k_ref, (1.0 - f).astype(BF16), HG_DK)
    store_heads(v_ref, _dot(h, w_ref[:, 2 * nk:2 * nk + nv]).astype(BF16), HG_DV)
    store_heads(sg_ref, _silu(_dot(h, w_ref[:, 2 * nk + nv:])).astype(BF16), HG_DV)


def _hgrn_chunk(q, kk, g, v, state_t, b_scr):
    c_rows = q.shape[0]
    row = lax.broadcasted_iota(jnp.int32, (c_rows, HG_DK), 0)
    trow = lax.broadcasted_iota(jnp.int32, (c_rows, c_rows), 0)
    tcol = lax.broadcasted_iota(jnp.int32, (c_rows, c_rows), 1)

    b = g
    sh = 1
    while sh < c_rows:
        b = b + jnp.where(row >= sh, pltpu.roll(b, sh, 0), 0.0)
        sh *= 2
    b_scr[...] = b

    attn = jnp.where(trow == tcol, _dot_nt(q.astype(BF16), kk.astype(BF16)), 0.0)
    half = 1
    while half < c_rows:
        right = (row & half) != 0
        if half == 1:
            arg = jnp.where(right, g, 0.0)
        elif half == 2:
            phase = row & 3
            arg = jnp.where(phase == 0, pltpu.roll(g, c_rows - 1, 0),
                            jnp.where(phase == 1, 0.0,
                                      jnp.where(phase == 2, g, g + pltpu.roll(g, 1, 0))))
        else:
            blk = 2 * half
            ref_rows = jnp.concatenate(
                [jnp.broadcast_to(b_scr[m * blk + half - 1:m * blk + half, :], (blk, HG_DK))
                 for m in range(c_rows // blk)], axis=0)
            diff = b - ref_rows
            arg = jnp.where(right, diff, -diff)
        x = jnp.exp(arg)
        q_l = jnp.where(right, q * x, 0.0).astype(BF16)
        k_l = jnp.where(right, 0.0, kk * x).astype(BF16)
        same_block = (trow & -(2 * half)) == (tcol & -(2 * half))
        attn = attn + jnp.where(same_block, _dot_nt(q_l, k_l), 0.0)
        half *= 2

    b_end = b_scr[c_rows - 1:c_rows, :]
    o = _dot(attn.astype(BF16), v)
    o = o + _dot_nt((q * jnp.exp(b)).astype(BF16), state_t.astype(BF16))
    k_dec = (kk * jnp.exp(b_end - b)).astype(BF16)
    new_state_t = jnp.exp(b_end) * state_t + _dot_tn(v, k_dec)
    return o, new_state_t


def _hgrn_rec_kernel(q_ref, k_ref, lf_ref, v_ref, sg_ref, g_ref, o_ref, state_scr, b_scr):
    @pl.when(pl.program_id(1) == 0)
    def _():
        state_scr[...] = jnp.zeros_like(state_scr)

    norm_g = g_ref[...]
    tb = q_ref.shape[2]

    def head(hd, carry):
        state_t = state_scr[hd]
        for c in range(tb // HG_KCHUNK):
            rows = pl.ds(c * HG_KCHUNK, HG_KCHUNK)
            o, state_t = _hgrn_chunk(q_ref[0, hd, rows, :].astype(F32), k_ref[0, hd, rows, :].astype(F32),
                                     lf_ref[0, hd, rows, :], v_ref[0, hd, rows, :], state_t, b_scr)
            o_ref[0, hd, rows, :] = (_rms(o, norm_g) * sg_ref[0, hd, rows, :].astype(F32)).astype(BF16)
        state_scr[hd] = state_t
        return carry

    lax.fori_loop(0, HG_HEADS, head, 0)


def _hgrn_mixer(x, mod, norm_g, hg_lb, layer, w_in, out_norm_g, w_o):
    batch, seq, d = x.shape
    nk = HG_HEADS * HG_DK
    nv = HG_HEADS * HG_DV
    tm = TM_PROJ
    const = lambda b, s: (0, 0)
    heads_shape = lambda w, dt: jax.ShapeDtypeStruct((batch, HG_HEADS, seq, w), dt)
    heads_blk = lambda rows, w: pl.BlockSpec((1, HG_HEADS, rows, w), lambda b, s: (b, 0, s, 0))
    q, k, lf, v, sg = pl.pallas_call(
        functools.partial(_hgrn_proj_kernel, layer=layer),
        out_shape=(heads_shape(HG_DK, BF16), heads_shape(HG_DK, BF16), heads_shape(HG_DK, F32),
                   heads_shape(HG_DV, BF16), heads_shape(HG_DV, BF16)),
        grid=(batch, seq // tm),
        in_specs=[
            pl.BlockSpec((1, tm, d), lambda b, s: (b, s, 0)),
            pl.BlockSpec((1, 6, d), lambda b, s: (b, 0, 0)),
            pl.BlockSpec((1, d), const),
            pl.BlockSpec((DEPTH, nk), const),
            pl.BlockSpec(w_in.shape, const),
        ],
        out_specs=(heads_blk(tm, HG_DK), heads_blk(tm, HG_DK), heads_blk(tm, HG_DK), heads_blk(tm, HG_DV),
                   heads_blk(tm, HG_DV)),
        compiler_params=_params(("parallel", "parallel")),
        name="hgrn_proj",
    )(x, mod, norm_g.reshape(1, d), hg_lb, w_in)

    tb = TB_HGRN
    o = pl.pallas_call(
        _hgrn_rec_kernel,
        out_shape=heads_shape(HG_DV, BF16),
        grid=(batch, seq // tb),
        in_specs=[heads_blk(tb, HG_DK), heads_blk(tb, HG_DK), heads_blk(tb, HG_DK), heads_blk(tb, HG_DV),
                  heads_blk(tb, HG_DV), pl.BlockSpec((1, HG_DV), const)],
        out_specs=heads_blk(tb, HG_DV),
        scratch_shapes=[pltpu.VMEM((HG_HEADS, HG_DV, HG_DK), F32), pltpu.VMEM((HG_KCHUNK, HG_DK), F32)],
        compiler_params=_params(("parallel", "arbitrary")),
        name="hgrn_recurrence",
    )(q, k, lf, v, sg, out_norm_g.reshape(1, HG_DV))
    return o, w_o


def kernel(x, c, positions, ada_w, ada_b, mix_norm_g, ffn_norm_g, gm_w_in, gm_ln_g, gm_ln_b, gm_w_s, gm_b_s,
           gm_w_out, mla_w_a, mla_q_norm_g, mla_kv_norm_g, mla_w_qb, mla_w_kvb, mla_w_o, hg_lb, hg_w_in,
           hg_norm_g, hg_w_o, ff_w_up, ff_conv_w, ff_conv_b, ff_w_down, final_g):
    mod_all = _adaln(c, ada_w, ada_b)
    gm_w_in, gm_w_out = gm_w_in.astype(BF16), gm_w_out.astype(BF16)
    ff_w_up, ff_w_down = ff_w_up.astype(BF16), ff_w_down.astype(BF16)
    for i in range(DEPTH):
        mod = mod_all[i]
        kind, j = i % N_MIXERS, i // N_MIXERS
        proj = None
        if kind == 0:
            x = _gmlp_mixer(x, mod, mix_norm_g[i], j, gm_w_in, gm_ln_g[j], gm_ln_b[j], gm_w_s, gm_b_s[j], gm_w_out)
        elif kind == 1:
            proj = _mla_mixer(x, positions, mod, mix_norm_g[i], mla_w_a[j], mla_q_norm_g[j], mla_kv_norm_g[j],
                              mla_w_qb[j], mla_w_kvb[j], mla_w_o[j])
        else:
            proj = _hgrn_mixer(x, mod, mix_norm_g[i], hg_lb, i, hg_w_in[j].astype(BF16), hg_norm_g[j],
                               hg_w_o[j].astype(BF16))
        x = _conv_ffn(x, mod, ffn_norm_g[i], i, ff_w_up, ff_conv_w, ff_conv_b, ff_w_down, final_g,
                      final_norm=(i == DEPTH - 1), proj=proj)
    return x
```

```python
import functools

import jax
import jax.numpy as jnp
from jax import lax
from jax.experimental import pallas as pl
from jax.experimental.pallas import tpu as pltpu

F32 = jnp.float32
BF16 = jnp.bfloat16

D_MODEL = 1024
DEPTH = 4
N_MIXERS = 3
EPS = 1e-6

GM_CHUNK = 128
GM_GROUPS = 8
GM_INNER = 2 * D_MODEL
GM_GROUP_DIM = GM_INNER // GM_GROUPS

MLA_HEADS = 8
MLA_Q_RANK = D_MODEL // 4
MLA_KV_RANK = D_MODEL // 8
MLA_NOPE = 128
MLA_ROPE = 64
MLA_V = 128
ROPE_THETA = 10000.0
ROPE_PACK = 128 // (MLA_ROPE // 2)

HG_HEADS = 8
HG_DK = 128
HG_DV = D_MODEL // HG_HEADS

FF_DIM = 2816
CONV_W = 3

LANES = 128
CONV_TAIL = 8
VMEM_LIMIT = 56 * 1024 * 1024

TM_FFN = 512
FC_FFN = 256
TM_GMLP = 512
TM_PROJ = 512
TQ_ATTN = 512
FLASH_HEADS = 4
TB_HGRN = 1024
HG_KCHUNK = 128
LOG2E = 1.4426950408889634
MLA_HEAD_PAD = MLA_NOPE + LANES


def _params(sem):
    return pltpu.CompilerParams(dimension_semantics=sem, vmem_limit_bytes=VMEM_LIMIT)


def _resident(arr, layer=None):
    if layer is None:
        shape, index = arr.shape, (0,) * arr.ndim
    else:
        shape, index = (None,) + arr.shape[1:], (layer,) + (0,) * (arr.ndim - 1)
    return pl.BlockSpec(shape, lambda *_: index, pipeline_mode=pl.Buffered(1))


def _rms(x, g):
    return x * lax.rsqrt(jnp.mean(x * x, axis=-1, keepdims=True) + EPS) * g


def _rms_mod(x, g, shift, scale):
    return _rms(x, g) * (1.0 + scale) + shift


def _dot(a, b):
    return jnp.dot(a, b, preferred_element_type=F32)


def _dot_nt(a, b):
    return lax.dot_general(a, b, (((1,), (1,)), ((), ())), preferred_element_type=F32)


def _dot_tn(a, b):
    return lax.dot_general(a, b, (((0,), (0,)), ((), ())), preferred_element_type=F32)


def _silu(x):
    return x / (1.0 + jnp.exp2(x * (-LOG2E)))


def _gelu_tanh(x):
    c1 = -2.0 * LOG2E * 0.7978845608028654
    return x / (1.0 + jnp.exp2(x * (c1 + (c1 * 0.044715) * (x * x))))


def _adaln_kernel(c_ref, w_ref, b_ref, o_ref):
    c_act = _silu(c_ref[...]).astype(BF16)
    o_ref[0] = _dot(c_act, w_ref[0].astype(BF16)) + b_ref[0]


def _adaln(c, ada_w, ada_b):
    batch = c.shape[0]
    rows = 8
    tn = 1536
    c_pad = jnp.pad(c, ((0, rows - batch), (0, 0)))
    n = 6 * D_MODEL
    mod = pl.pallas_call(
        _adaln_kernel,
        out_shape=jax.ShapeDtypeStruct((DEPTH, rows, n), F32),
        grid=(DEPTH, n // tn),
        in_specs=[
            pl.BlockSpec((rows, D_MODEL), lambda i, j: (0, 0)),
            pl.BlockSpec((1, D_MODEL, tn), lambda i, j: (i, 0, j)),
            pl.BlockSpec((1, 1, tn), lambda i, j: (i, 0, j)),
        ],
        out_specs=pl.BlockSpec((1, rows, tn), lambda i, j: (i, 0, j)),
        compiler_params=_params(("parallel", "parallel")),
        name="adaln_mod",
    )(c_pad, ada_w, ada_b.reshape(DEPTH, 1, n))
    return mod[:, :batch].reshape(DEPTH, batch, 6, D_MODEL)


def _ffn_kernel(*refs, final_norm, fused_proj):
    if fused_proj:
        a_ref, wo_ref, *refs = refs
    x_ref, mod_ref, ng_ref, wu_ref, cw_ref, cb_ref, wd_ref, fg_ref, o_ref, h_scr, act_scr, tail_scr, x_scr = refs
    mod = mod_ref[0]
    tm = x_ref.shape[1]
    tail = tail_scr.shape[1]
    n_chunks = FF_DIM // FC_FFN

    x = x_ref[0]
    if fused_proj:
        if len(a_ref.shape) == 4:
            a = jnp.concatenate([a_ref[0, hd] for hd in range(a_ref.shape[1])], axis=-1)
        else:
            a = a_ref[0]
        x = x + (1.0 + mod[2:3]) * _dot(a, wo_ref[...])
        x_scr[...] = x
    h_scr[...] = _rms_mod(x, ng_ref[...], mod[3:4], mod[4:5]).astype(BF16)
    h = h_scr[...]

    @pl.when(pl.program_id(1) == 0)
    def _():
        tail_scr[...] = jnp.zeros_like(tail_scr)

    def conv(ci):
        lo = ci * FC_FFN
        a = _dot(h, wu_ref[:, lo:lo + FC_FFN])
        ext = jnp.concatenate([tail_scr[ci], a], axis=0)
        tail_scr[ci] = a[tm - tail:]
        cw = cw_ref[:, lo:lo + FC_FFN]
        return (cb_ref[:, lo:lo + FC_FFN] + cw[2:3] * a + cw[1:2] * pltpu.roll(ext, 1, 0)[tail:]
                + cw[0:1] * pltpu.roll(ext, 2, 0)[tail:])

    for c in range(n_chunks):
        act_scr[:, c * FC_FFN:(c + 1) * FC_FFN] = (_silu(conv(c)) * conv(n_chunks + c)).astype(BF16)

    x = x_scr[...] if fused_proj else x_ref[0]
    out = x + (1.0 + mod[5:6]) * _dot(act_scr[...], wd_ref[...])
    if final_norm:
        out = _rms(out, fg_ref[...])
    o_ref[0] = out


def _conv_ffn(x, mod, norm_g, layer, w_up, conv_w, conv_b, w_down, final_g, final_norm, proj=None):
    batch, seq, d = x.shape
    tm = TM_FFN
    const = lambda b, s: (0, 0)
    conv_b = conv_b.reshape(DEPTH, 1, -1)
    tok = pl.BlockSpec((1, tm, d), lambda b, s: (b, s, 0))
    operands, in_specs = [], []
    if proj is not None:
        a, w_o = proj
        if a.ndim == 4:
            in_specs.append(pl.BlockSpec((1, a.shape[1], tm, a.shape[3]), lambda b, s: (b, 0, s, 0)))
        else:
            in_specs.append(pl.BlockSpec((1, tm, a.shape[2]), lambda b, s: (b, s, 0)))
        in_specs.append(_resident(w_o))
        operands += [a, w_o]
    operands += [x, mod, norm_g.reshape(1, d), w_up, conv_w, conv_b, w_down, final_g.reshape(1, d)]
    in_specs += [tok, pl.BlockSpec((1, 6, d), lambda b, s: (b, 0, 0)), pl.BlockSpec((1, d), const),
                 _resident(w_up, layer), _resident(conv_w, layer), _resident(conv_b, layer),
                 _resident(w_down, layer), pl.BlockSpec((1, d), const)]
    return pl.pallas_call(
        functools.partial(_ffn_kernel, final_norm=final_norm, fused_proj=proj is not None),
        out_shape=jax.ShapeDtypeStruct(x.shape, F32),
        grid=(batch, seq // tm),
        in_specs=in_specs,
        out_specs=tok,
        scratch_shapes=[pltpu.VMEM((tm, d), BF16), pltpu.VMEM((tm, FF_DIM), BF16),
                        pltpu.VMEM((2 * FF_DIM // FC_FFN, CONV_TAIL, FC_FFN), F32),
                        pltpu.VMEM((tm, d) if proj is not None else (8, LANES), F32)],
        compiler_params=_params(("parallel", "arbitrary")),
        name="conv_ffn",
    )(*operands)


def _gmlp_kernel(x_ref, mod_ref, ng_ref, win_ref, lng_ref, lnb_ref, ws_ref, bs_ref, wout_ref, o_ref, v_scr,
                 y_scr):
    mod = mod_ref[0]
    x = x_ref[0]
    tm = x.shape[0]
    h = _rms_mod(x, ng_ref[...], mod[0:1], mod[1:2]).astype(BF16)

    v = _gelu_tanh(_dot(h, win_ref[:, GM_INNER:]))
    mu = jnp.mean(v, axis=-1, keepdims=True)
    vc = v - mu
    var = jnp.mean(vc * vc, axis=-1, keepdims=True)
    v_scr[...] = (vc * lax.rsqrt(var + EPS) * lng_ref[...] + lnb_ref[...]).astype(BF16)

    row = lax.broadcasted_iota(jnp.int32, (GM_CHUNK, GM_CHUNK), 0)
    col = lax.broadcasted_iota(jnp.int32, (GM_CHUNK, GM_CHUNK), 1)
    causal = row >= col

    for g in range(GM_GROUPS):
        lo, hi = g * GM_GROUP_DIM, (g + 1) * GM_GROUP_DIM
        u = _gelu_tanh(_dot(h, win_ref[:, lo:hi]))
        w_s = jnp.where(causal, ws_ref[g], 0.0).astype(BF16)
        bias = bs_ref[:, lo:hi]
        for c in range(tm // GM_CHUNK):
            rows = slice(c * GM_CHUNK, (c + 1) * GM_CHUNK)
            sv = _dot(w_s, v_scr[rows, lo:hi]) + bias
            y_scr[rows, lo:hi] = (u[rows] * sv).astype(BF16)
    o_ref[0] = x + (1.0 + mod[2:3]) * _dot(y_scr[...], wout_ref[...])


def _gmlp_mixer(x, mod, norm_g, layer, w_in, ln_g, ln_b, w_s, b_s, w_out):
    batch, seq, d = x.shape
    tm = TM_GMLP
    bias = jnp.repeat(b_s.T, GM_GROUP_DIM, axis=1)
    const = lambda b, s: (0, 0)
    return pl.pallas_call(
        _gmlp_kernel,
        out_shape=jax.ShapeDtypeStruct(x.shape, F32),
        grid=(batch, seq // tm),
        in_specs=[
            pl.BlockSpec((1, tm, d), lambda b, s: (b, s, 0)),
            pl.BlockSpec((1, 6, d), lambda b, s: (b, 0, 0)),
            pl.BlockSpec((1, d), const),
            _resident(w_in, layer),
            pl.BlockSpec((1, GM_INNER), const),
            pl.BlockSpec((1, GM_INNER), const),
            _resident(w_s, layer),
            _resident(bias),
            _resident(w_out, layer),
        ],
        out_specs=pl.BlockSpec((1, tm, d), lambda b, s: (b, s, 0)),
        scratch_shapes=[pltpu.VMEM((tm, GM_INNER), BF16), pltpu.VMEM((tm, GM_INNER), BF16)],
        compiler_params=_params(("parallel", "parallel")),
        name="gmlp_mixer",
    )(x, mod, norm_g.reshape(1, d), w_in, ln_g.reshape(1, -1), ln_b.reshape(1, -1), w_s, bias, w_out)


def _mla_proj_kernel(x_ref, pos_ref, mod_ref, ng_ref, invf_ref, spread_ref, wa_ref, qg_ref, kvg_ref, wq_ref, wqs_ref,
                     wk_ref, wv_ref, q_ref, k_ref, v_ref, cos_scr, sin_scr, *, scale):
    mod = mod_ref[0]
    h = _rms_mod(x_ref[0], ng_ref[...], mod[0:1], mod[1:2]).astype(BF16)
    a = _dot(h, wa_ref[...])
    cq = _rms(a[:, :MLA_Q_RANK], qg_ref[...]).astype(BF16)
    kv_lo = MLA_Q_RANK
    ckv = _rms(a[:, kv_lo:kv_lo + MLA_KV_RANK], kvg_ref[...]).astype(BF16)
    kr_lo = kv_lo + MLA_KV_RANK
    k_rope, k_rope_sw = a[:, kr_lo:kr_lo + LANES], a[:, kr_lo + LANES:kr_lo + 2 * LANES]

    ang = pos_ref[0] * invf_ref[...]
    n = ang.shape[0]

    def spread(table, scr):
        hi = table.astype(BF16)
        rest = table - hi.astype(F32)
        mid = rest.astype(BF16)
        lo = (rest - mid.astype(F32)).astype(BF16)
        y = _dot(jnp.concatenate([hi, mid, lo], axis=0), spread_ref[...])
        z = y[0:n] + y[n:2 * n] + y[2 * n:3 * n]
        for j in range(ROPE_PACK):
            scr[pl.ds(j, n, stride=ROPE_PACK), :] = z[:, j * LANES:(j + 1) * LANES]
        return scr[...]

    cos, sin = spread(jnp.cos(ang), cos_scr), spread(jnp.sin(ang), sin_scr)

    k_r = (k_rope * cos + k_rope_sw * sin).astype(BF16)
    for hp in range(MLA_HEADS // 2):
        q_main = _dot(cq, wq_ref[:, 2 * hp * MLA_HEAD_PAD:2 * (hp + 1) * MLA_HEAD_PAD])
        q_sw = _dot(cq, wqs_ref[:, 2 * hp * LANES:2 * (hp + 1) * LANES])
        k_nope = _dot(ckv, wk_ref[:, 2 * hp * MLA_NOPE:2 * (hp + 1) * MLA_NOPE])
        v_ref[0, :, 2 * hp * MLA_V:2 * (hp + 1) * MLA_V] = _dot(
            ckv, wv_ref[:, 2 * hp * MLA_V:2 * (hp + 1) * MLA_V]).astype(BF16)
        for i in range(2):
            lo = (2 * hp + i) * MLA_HEAD_PAD
            mid = lo + MLA_NOPE
            src = i * MLA_HEAD_PAD
            q_ref[0, :, lo:mid] = (q_main[:, src:src + MLA_NOPE] * scale).astype(BF16)
            q_r = (q_main[:, src + MLA_NOPE:src + MLA_HEAD_PAD] * cos
                   + q_sw[:, i * LANES:(i + 1) * LANES] * sin)
            q_ref[0, :, mid:mid + LANES] = (q_r * scale).astype(BF16)
            k_ref[0, :, lo:mid] = k_nope[:, i * MLA_NOPE:(i + 1) * MLA_NOPE].astype(BF16)
            k_ref[0, :, mid:mid + LANES] = k_r


def _flash_kernel(q_ref, k_ref, v_ref, o_ref, m_scr, l_scr, acc_scr):
    qi = pl.program_id(2)
    t = q_ref.shape[1]

    m_scr[...] = jnp.full_like(m_scr, -jnp.inf)
    l_scr[...] = jnp.zeros_like(l_scr)
    acc_scr[...] = jnp.zeros_like(acc_scr)

    def block(j, masked):
        rows = pl.ds(pl.multiple_of(j * t, t), t)
        for hd in range(FLASH_HEADS):
            qk_lanes = slice(hd * MLA_HEAD_PAD, (hd + 1) * MLA_HEAD_PAD)
            v_lanes = slice(hd * MLA_V, (hd + 1) * MLA_V)
            s = _dot_nt(q_ref[0, :, qk_lanes], k_ref[0, rows, qk_lanes])
            if masked:
                row = lax.broadcasted_iota(jnp.int32, (t, t), 0)
                col = lax.broadcasted_iota(jnp.int32, (t, t), 1)
                s = jnp.where(row >= col, s, -jnp.inf)
            m_prev = m_scr[hd]
            m_next = jnp.maximum(m_prev, jnp.max(s, axis=1, keepdims=True))
            p = jnp.exp2(s - jnp.concatenate([m_next] * (t // LANES), axis=1))
            alpha = jnp.exp2(m_prev - m_next)
            l_scr[hd] = alpha * l_scr[hd] + jnp.sum(p, axis=1, keepdims=True)
            acc_scr[hd] = alpha * acc_scr[hd] + _dot(p.astype(BF16), v_ref[0, rows, v_lanes])
            m_scr[hd] = m_next

    def unmasked(j, carry):
        block(j, False)
        return carry

    lax.fori_loop(0, qi, unmasked, 0)
    block(qi, True)
    for hd in range(FLASH_HEADS):
        o_ref[0, :, hd * MLA_V:(hd + 1) * MLA_V] = (acc_scr[hd] / l_scr[hd]).astype(BF16)


def _mla_mixer(x, positions, mod, norm_g, w_a, q_norm_g, kv_norm_g, w_qb, w_kvb, w_o):
    batch, seq, d = x.shape
    heads, rope, half = MLA_HEADS, MLA_ROPE, MLA_ROPE // 2
    pad = LANES - rope

    def swap_halves(w):
        return jnp.concatenate([-w[..., half:], w[..., :half]], axis=-1)

    def pad_lanes(w):
        return jnp.pad(w, [(0, 0)] * (w.ndim - 1) + [(0, pad)])

    kr_lo = MLA_Q_RANK + MLA_KV_RANK
    w_kr = w_a[:, kr_lo:]
    w_a_ext = jnp.concatenate([w_a[:, :kr_lo], pad_lanes(w_kr), pad_lanes(swap_halves(w_kr))], axis=1)

    wq = w_qb.reshape(MLA_Q_RANK, heads, MLA_NOPE + rope)
    wq_rope = wq[:, :, MLA_NOPE:]
    w_q_main = jnp.concatenate([wq[:, :, :MLA_NOPE], pad_lanes(wq_rope)], axis=-1).reshape(MLA_Q_RANK, -1)
    w_q_sw = pad_lanes(swap_halves(wq_rope)).reshape(MLA_Q_RANK, -1)

    wkv = w_kvb.reshape(MLA_KV_RANK, heads, MLA_NOPE + MLA_V)
    w_k = wkv[:, :, :MLA_NOPE].reshape(MLA_KV_RANK, -1)
    w_v = wkv[:, :, MLA_NOPE:].reshape(MLA_KV_RANK, -1)

    inv_freq = ROPE_THETA ** (-jnp.arange(0, rope, 2, dtype=F32) / rope)
    inv_freq = jnp.tile(inv_freq, ROPE_PACK).reshape(1, LANES)
    pos = jnp.repeat(positions.astype(F32).reshape(batch, seq // ROPE_PACK, ROPE_PACK), half, axis=2)
    lane = jnp.arange(LANES)
    spread = jnp.concatenate([lane[:, None] == half * j + lane[None, :] % half for j in range(ROPE_PACK)],
                             axis=1).astype(BF16)

    tm = TM_PROJ
    qk_w = heads * MLA_HEAD_PAD
    const = lambda b, s: (0, 0)
    full = lambda arr: pl.BlockSpec(arr.shape, const)
    weights = [w_a_ext.astype(BF16), q_norm_g.reshape(1, -1), kv_norm_g.reshape(1, -1), w_q_main.astype(BF16),
               w_q_sw.astype(BF16), w_k.astype(BF16), w_v.astype(BF16)]
    q, k, v = pl.pallas_call(
        functools.partial(_mla_proj_kernel, scale=LOG2E * (MLA_NOPE + rope) ** -0.5),
        out_shape=(jax.ShapeDtypeStruct((batch, seq, qk_w), BF16),
                   jax.ShapeDtypeStruct((batch, seq, qk_w), BF16),
                   jax.ShapeDtypeStruct((batch, seq, heads * MLA_V), BF16)),
        grid=(batch, seq // tm),
        in_specs=[
            pl.BlockSpec((1, tm, d), lambda b, s: (b, s, 0)),
            pl.BlockSpec((1, tm // ROPE_PACK, LANES), lambda b, s: (b, s, 0)),
            pl.BlockSpec((1, 6, d), lambda b, s: (b, 0, 0)),
            pl.BlockSpec((1, d), const),
            pl.BlockSpec((1, LANES), const),
            full(spread),
        ] + [full(w) for w in weights],
        out_specs=(pl.BlockSpec((1, tm, qk_w), lambda b, s: (b, s, 0)),
                   pl.BlockSpec((1, tm, qk_w), lambda b, s: (b, s, 0)),
                   pl.BlockSpec((1, tm, heads * MLA_V), lambda b, s: (b, s, 0))),
        scratch_shapes=[pltpu.VMEM((tm, LANES), F32), pltpu.VMEM((tm, LANES), F32)],
        compiler_params=_params(("parallel", "parallel")),
        name="mla_proj",
    )(x, pos, mod, norm_g.reshape(1, d), inv_freq, spread, *weights)

    t = TQ_ATTN
    fh = FLASH_HEADS
    o = pl.pallas_call(
        _flash_kernel,
        out_shape=jax.ShapeDtypeStruct((batch, seq, heads * MLA_V), BF16),
        grid=(batch, heads // fh, seq // t),
        in_specs=[
            pl.BlockSpec((1, t, fh * MLA_HEAD_PAD), lambda b, h, i: (b, i, h)),
            pl.BlockSpec((1, seq, fh * MLA_HEAD_PAD), lambda b, h, i: (b, 0, h)),
            pl.BlockSpec((1, seq, fh * MLA_V), lambda b, h, i: (b, 0, h)),
        ],
        out_specs=pl.BlockSpec((1, t, fh * MLA_V), lambda b, h, i: (b, i, h)),
        scratch_shapes=[pltpu.VMEM((fh, t, LANES), F32), pltpu.VMEM((fh, t, LANES), F32),
                        pltpu.VMEM((fh, t, MLA_V), F32)],
        compiler_params=_params(("parallel", "parallel", "arbitrary")),
        name="mla_flash",
    )(q, k, v)
    return o, w_o.astype(BF16)


def _hgrn_proj_kernel(x_ref, mod_ref, ng_ref, lbp_ref, w_ref, q_ref, k_ref, lf_ref, v_ref, sg_ref, *, layer):
    mod = mod_ref[0]
    h = _rms_mod(x_ref[0], ng_ref[...], mod[0:1], mod[1:2]).astype(BF16)
    nk = HG_HEADS * HG_DK
    nv = HG_HEADS * HG_DV

    lbp = lbp_ref[...]
    e = jnp.exp(lbp - jnp.max(lbp, axis=0, keepdims=True))
    p = e / jnp.sum(e, axis=0, keepdims=True)
    lb = jnp.sum(p[1:layer + 1], axis=0, keepdims=True) if layer >= 1 else jnp.zeros((1, nk), F32)

    def store_heads(ref, val, hd0):
        for i in range(val.shape[1] // HG_DK):
            ref[0, hd0 + i] = val[:, i * HG_DK:(i + 1) * HG_DK]

    pair = 2 * HG_DK
    for hp in range(HG_HEADS // 2):
        cols = lambda seg: slice(seg * nk + hp * pair, seg * nk + (hp + 1) * pair)
        store_heads(q_ref, _dot(h, w_ref[:, cols(0)]).astype(BF16), 2 * hp)
        lb_p = lb[:, hp * pair:(hp + 1) * pair]
        f = lb_p + (1.0 - lb_p) * jax.nn.sigmoid(_dot(h, w_ref[:, cols(1)]))
        store_heads(lf_ref, jnp.log(f), 2 * hp)
        store_heads(k_ref, (1.0 - f).astype(BF16), 2 * hp)
        store_heads(v_ref, _dot(h, w_ref[:, cols(2)]).astype(BF16), 2 * hp)
        store_heads(sg_ref, _silu(_dot(h, w_ref[:, cols(3)])).astype(BF16), 2 * hp)


def _hgrn_chunk(q, kk, g, v, state_t, b_scr):
    c_rows = q.shape[0]
    row = lax.broadcasted_iota(jnp.int32, (c_rows, HG_DK), 0)
    trow = lax.broadcasted_iota(jnp.int32, (c_rows, c_rows), 0)
    tcol = lax.broadcasted_iota(jnp.int32, (c_rows, c_rows), 1)

    b = g
    sh = 1
    while sh < c_rows:
        b = b + jnp.where(row >= sh, pltpu.roll(b, sh, 0), 0.0)
        sh *= 2
    b_scr[...] = b

    attn = jnp.where(trow == tcol, _dot_nt(q.astype(BF16), kk.astype(BF16)), 0.0)
    half = 1
    while half < c_rows:
        right = (row & half) != 0
        if half == 1:
            arg = jnp.where(right, g, 0.0)
        elif half == 2:
            phase = row & 3
            arg = jnp.where(phase == 0, pltpu.roll(g, c_rows - 1, 0),
                            jnp.where(phase == 1, 0.0,
                                      jnp.where(phase == 2, g, g + pltpu.roll(g, 1, 0))))
        else:
            blk = 2 * half
            ref_rows = jnp.concatenate(
                [jnp.broadcast_to(b_scr[m * blk + half - 1:m * blk + half, :], (blk, HG_DK))
                 for m in range(c_rows // blk)], axis=0)
            diff = b - ref_rows
            arg = jnp.where(right, diff, -diff)
        x = jnp.exp(arg)
        q_l = jnp.where(right, q * x, 0.0).astype(BF16)
        k_l = jnp.where(right, 0.0, kk * x).astype(BF16)
        same_block = (trow & -(2 * half)) == (tcol & -(2 * half))
        attn = attn + jnp.where(same_block, _dot_nt(q_l, k_l), 0.0)
        half *= 2

    b_end = b_scr[c_rows - 1:c_rows, :]
    o = _dot(attn.astype(BF16), v)
    o = o + _dot_nt((q * jnp.exp(b)).astype(BF16), state_t.astype(BF16))
    k_dec = (kk * jnp.exp(b_end - b)).astype(BF16)
    new_state_t = jnp.exp(b_end) * state_t + _dot_tn(v, k_dec)
    return o, new_state_t


def _hgrn_rec_kernel(q_ref, k_ref, lf_ref, v_ref, sg_ref, g_ref, o_ref, state_scr, b_scr):
    @pl.when(pl.program_id(1) == 0)
    def _():
        state_scr[...] = jnp.zeros_like(state_scr)

    norm_g = g_ref[...]
    tb = q_ref.shape[2]

    def head(hd, carry):
        state_t = state_scr[hd]
        for c in range(tb // HG_KCHUNK):
            rows = pl.ds(c * HG_KCHUNK, HG_KCHUNK)
            o, state_t = _hgrn_chunk(q_ref[0, hd, rows, :].astype(F32), k_ref[0, hd, rows, :].astype(F32),
                                     lf_ref[0, hd, rows, :], v_ref[0, hd, rows, :], state_t, b_scr.at[c])
            o_ref[0, hd, rows, :] = (_rms(o, norm_g) * sg_ref[0, hd, rows, :].astype(F32)).astype(BF16)
        state_scr[hd] = state_t
        return carry

    lax.fori_loop(0, HG_HEADS, head, 0)


def _hgrn_mixer(x, mod, norm_g, hg_lb, layer, w_in, out_norm_g, w_o):
    batch, seq, d = x.shape
    nk = HG_HEADS * HG_DK
    nv = HG_HEADS * HG_DV
    tm = TM_PROJ
    const = lambda b, s: (0, 0)
    heads_shape = lambda w, dt: jax.ShapeDtypeStruct((batch, HG_HEADS, seq, w), dt)
    heads_blk = lambda rows, w: pl.BlockSpec((1, HG_HEADS, rows, w), lambda b, s: (b, 0, s, 0))
    q, k, lf, v, sg = pl.pallas_call(
        functools.partial(_hgrn_proj_kernel, layer=layer),
        out_shape=(heads_shape(HG_DK, BF16), heads_shape(HG_DK, BF16), heads_shape(HG_DK, F32),
                   heads_shape(HG_DV, BF16), heads_shape(HG_DV, BF16)),
        grid=(batch, seq // tm),
        in_specs=[
            pl.BlockSpec((1, tm, d), lambda b, s: (b, s, 0)),
            pl.BlockSpec((1, 6, d), lambda b, s: (b, 0, 0)),
            pl.BlockSpec((1, d), const),
            pl.BlockSpec((DEPTH, nk), const),
            pl.BlockSpec(w_in.shape, const),
        ],
        out_specs=(heads_blk(tm, HG_DK), heads_blk(tm, HG_DK), heads_blk(tm, HG_DK), heads_blk(tm, HG_DV),
                   heads_blk(tm, HG_DV)),
        compiler_params=_params(("parallel", "parallel")),
        name="hgrn_proj",
    )(x, mod, norm_g.reshape(1, d), hg_lb, w_in)

    tb = TB_HGRN
    o = pl.pallas_call(
        _hgrn_rec_kernel,
        out_shape=heads_shape(HG_DV, BF16),
        grid=(batch, seq // tb),
        in_specs=[heads_blk(tb, HG_DK), heads_blk(tb, HG_DK), heads_blk(tb, HG_DK), heads_blk(tb, HG_DV),
                  heads_blk(tb, HG_DV), pl.BlockSpec((1, HG_DV), const)],
        out_specs=heads_blk(tb, HG_DV),
        scratch_shapes=[pltpu.VMEM((HG_HEADS, HG_DV, HG_DK), F32),
                        pltpu.VMEM((tb // HG_KCHUNK, HG_KCHUNK, HG_DK), F32)],
        compiler_params=_params(("parallel", "arbitrary")),
        name="hgrn_recurrence",
    )(q, k, lf, v, sg, out_norm_g.reshape(1, HG_DV))
    return o, w_o


def kernel(x, c, positions, ada_w, ada_b, mix_norm_g, ffn_norm_g, gm_w_in, gm_ln_g, gm_ln_b, gm_w_s, gm_b_s,
           gm_w_out, mla_w_a, mla_q_norm_g, mla_kv_norm_g, mla_w_qb, mla_w_kvb, mla_w_o, hg_lb, hg_w_in,
           hg_norm_g, hg_w_o, ff_w_up, ff_conv_w, ff_conv_b, ff_w_down, final_g):
    mod_all = _adaln(c, ada_w, ada_b)
    gm_w_in, gm_w_out = gm_w_in.astype(BF16), gm_w_out.astype(BF16)
    ff_w_up, ff_w_down = ff_w_up.astype(BF16), ff_w_down.astype(BF16)
    for i in range(DEPTH):
        mod = mod_all[i]
        kind, j = i % N_MIXERS, i // N_MIXERS
        proj = None
        if kind == 0:
            x = _gmlp_mixer(x, mod, mix_norm_g[i], j, gm_w_in, gm_ln_g[j], gm_ln_b[j], gm_w_s, gm_b_s[j], gm_w_out)
        elif kind == 1:
            proj = _mla_mixer(x, positions, mod, mix_norm_g[i], mla_w_a[j], mla_q_norm_g[j], mla_kv_norm_g[j],
                              mla_w_qb[j], mla_w_kvb[j], mla_w_o[j])
        else:
            proj = _hgrn_mixer(x, mod, mix_norm_g[i], hg_lb, i, hg_w_in[j].astype(BF16), hg_norm_g[j],
                               hg_w_o[j].astype(BF16))
        x = _conv_ffn(x, mod, ffn_norm_g[i], i, ff_w_up, ff_conv_w, ff_conv_b, ff_w_down, final_g,
                      final_norm=(i == DEPTH - 1), proj=proj)
    return x
```

```python
import functools

import jax
import jax.numpy as jnp
from jax import lax
from jax.experimental import pallas as pl
from jax.experimental.pallas import tpu as pltpu

F32 = jnp.float32
BF16 = jnp.bfloat16

D_MODEL = 1024
DEPTH = 4
N_MIXERS = 3
EPS = 1e-6

GM_CHUNK = 128
GM_GROUPS = 8
GM_INNER = 2 * D_MODEL
GM_GROUP_DIM = GM_INNER // GM_GROUPS

MLA_HEADS = 8
MLA_Q_RANK = D_MODEL // 4
MLA_KV_RANK = D_MODEL // 8
MLA_NOPE = 128
MLA_ROPE = 64
MLA_V = 128
ROPE_THETA = 10000.0
ROPE_PACK = 128 // (MLA_ROPE // 2)

HG_HEADS = 8
HG_DK = 128
HG_DV = D_MODEL // HG_HEADS

FF_DIM = 2816
CONV_W = 3

LANES = 128
CONV_TAIL = 8
VMEM_LIMIT = 56 * 1024 * 1024

TM_FFN = 512
FC_FFN = 256
TM_GMLP = 512
TM_PROJ = 512
TQ_ATTN = 512
FLASH_WIDE = 4
FLASH_HEADS = 4
TB_HGRN = 1024
HG_KCHUNK = 128
LOG2E = 1.4426950408889634
MLA_HEAD_PAD = MLA_NOPE + LANES


def _params(sem):
    return pltpu.CompilerParams(dimension_semantics=sem, vmem_limit_bytes=VMEM_LIMIT)


def _resident(arr, layer=None):
    if layer is None:
        shape, index = arr.shape, (0,) * arr.ndim
    else:
        shape, index = (None,) + arr.shape[1:], (layer,) + (0,) * (arr.ndim - 1)
    return pl.BlockSpec(shape, lambda *_: index, pipeline_mode=pl.Buffered(1))


def _rms(x, g):
    return x * lax.rsqrt(jnp.mean(x * x, axis=-1, keepdims=True) + EPS) * g


def _rms_mod(x, g, shift, scale):
    return _rms(x, g) * (1.0 + scale) + shift


def _dot(a, b):
    return jnp.dot(a, b, preferred_element_type=F32)


def _dot_nt(a, b):
    return lax.dot_general(a, b, (((1,), (1,)), ((), ())), preferred_element_type=F32)


def _dot_tn(a, b):
    return lax.dot_general(a, b, (((0,), (0,)), ((), ())), preferred_element_type=F32)


def _silu(x):
    return x / (1.0 + jnp.exp2(x * (-LOG2E)))


def _gelu_tanh(x):
    c1 = -2.0 * LOG2E * 0.7978845608028654
    return x / (1.0 + jnp.exp2(x * (c1 + (c1 * 0.044715) * (x * x))))


def _adaln_kernel(c_ref, w_ref, b_ref, o_ref):
    c_act = _silu(c_ref[...]).astype(BF16)
    o_ref[0] = _dot(c_act, w_ref[0].astype(BF16)) + b_ref[0]


def _adaln(c, ada_w, ada_b):
    batch = c.shape[0]
    rows = 8
    tn = 1536
    c_pad = jnp.pad(c, ((0, rows - batch), (0, 0)))
    n = 6 * D_MODEL
    mod = pl.pallas_call(
        _adaln_kernel,
        out_shape=jax.ShapeDtypeStruct((DEPTH, rows, n), F32),
        grid=(DEPTH, n // tn),
        in_specs=[
            pl.BlockSpec((rows, D_MODEL), lambda i, j: (0, 0)),
            pl.BlockSpec((1, D_MODEL, tn), lambda i, j: (i, 0, j)),
            pl.BlockSpec((1, 1, tn), lambda i, j: (i, 0, j)),
        ],
        out_specs=pl.BlockSpec((1, rows, tn), lambda i, j: (i, 0, j)),
        compiler_params=_params(("parallel", "parallel")),
        name="adaln_mod",
    )(c_pad, ada_w, ada_b.reshape(DEPTH, 1, n))
    return mod[:, :batch].reshape(DEPTH, batch, 6, D_MODEL)


def _ffn_kernel(*refs, final_norm, fused_proj):
    if fused_proj:
        a_ref, wo_ref, *refs = refs
    x_ref, mod_ref, ng_ref, wu_ref, cw_ref, cb_ref, wd_ref, fg_ref, o_ref, h_scr, act_scr, tail_scr, x_scr = refs
    mod = mod_ref[0]
    tm = x_ref.shape[1]
    tail = tail_scr.shape[1]
    n_chunks = FF_DIM // FC_FFN

    x = x_ref[0]
    if fused_proj:
        if len(a_ref.shape) == 4:
            a = jnp.concatenate([a_ref[0, hd] for hd in range(a_ref.shape[1])], axis=-1)
        else:
            a = a_ref[0]
        x = x + (1.0 + mod[2:3]) * _dot(a, wo_ref[...])
        x_scr[...] = x
    h_scr[...] = _rms_mod(x, ng_ref[...], mod[3:4], mod[4:5]).astype(BF16)
    h = h_scr[...]

    @pl.when(pl.program_id(1) == 0)
    def _():
        tail_scr[...] = jnp.zeros_like(tail_scr)

    def conv(ci):
        lo = ci * FC_FFN
        a = _dot(h, wu_ref[:, lo:lo + FC_FFN])
        ext = jnp.concatenate([tail_scr[ci], a], axis=0)
        tail_scr[ci] = a[tm - tail:]
        cw = cw_ref[:, lo:lo + FC_FFN]
        return (cb_ref[:, lo:lo + FC_FFN] + cw[2:3] * a + cw[1:2] * pltpu.roll(ext, 1, 0)[tail:]
                + cw[0:1] * pltpu.roll(ext, 2, 0)[tail:])

    for c in range(n_chunks):
        act_scr[:, c * FC_FFN:(c + 1) * FC_FFN] = (_silu(conv(c)) * conv(n_chunks + c)).astype(BF16)

    x = x_scr[...] if fused_proj else x_ref[0]
    out = x + (1.0 + mod[5:6]) * _dot(act_scr[...], wd_ref[...])
    if final_norm:
        out = _rms(out, fg_ref[...])
    o_ref[0] = out


def _conv_ffn(x, mod, norm_g, layer, w_up, conv_w, conv_b, w_down, final_g, final_norm, proj=None):
    batch, seq, d = x.shape
    tm = TM_FFN
    const = lambda b, s: (0, 0)
    conv_b = conv_b.reshape(DEPTH, 1, -1)
    tok = pl.BlockSpec((1, tm, d), lambda b, s: (b, s, 0))
    operands, in_specs = [], []
    if proj is not None:
        a, w_o = proj
        if a.ndim == 4:
            in_specs.append(pl.BlockSpec((1, a.shape[1], tm, a.shape[3]), lambda b, s: (b, 0, s, 0)))
        else:
            in_specs.append(pl.BlockSpec((1, tm, a.shape[2]), lambda b, s: (b, s, 0)))
        in_specs.append(_resident(w_o))
        operands += [a, w_o]
    operands += [x, mod, norm_g.reshape(1, d), w_up, conv_w, conv_b, w_down, final_g.reshape(1, d)]
    in_specs += [tok, pl.BlockSpec((1, 6, d), lambda b, s: (b, 0, 0)), pl.BlockSpec((1, d), const),
                 _resident(w_up, layer), _resident(conv_w, layer), _resident(conv_b, layer),
                 _resident(w_down, layer), pl.BlockSpec((1, d), const)]
    return pl.pallas_call(
        functools.partial(_ffn_kernel, final_norm=final_norm, fused_proj=proj is not None),
        out_shape=jax.ShapeDtypeStruct(x.shape, F32),
        grid=(batch, seq // tm),
        in_specs=in_specs,
        out_specs=tok,
        scratch_shapes=[pltpu.VMEM((tm, d), BF16), pltpu.VMEM((tm, FF_DIM), BF16),
                        pltpu.VMEM((2 * FF_DIM // FC_FFN, CONV_TAIL, FC_FFN), F32),
                        pltpu.VMEM((tm, d) if proj is not None else (8, LANES), F32)],
        compiler_params=_params(("parallel", "arbitrary")),
        name="conv_ffn",
    )(*operands)


def _gmlp_kernel(x_ref, mod_ref, ng_ref, win_ref, lng_ref, lnb_ref, ws_ref, bs_ref, wout_ref, o_ref, v_scr,
                 y_scr):
    mod = mod_ref[0]
    x = x_ref[0]
    tm = x.shape[0]
    h = _rms_mod(x, ng_ref[...], mod[0:1], mod[1:2]).astype(BF16)

    v = _gelu_tanh(_dot(h, win_ref[:, GM_INNER:]))
    mu = jnp.mean(v, axis=-1, keepdims=True)
    vc = v - mu
    var = jnp.mean(vc * vc, axis=-1, keepdims=True)
    v_scr[...] = (vc * lax.rsqrt(var + EPS) * lng_ref[...] + lnb_ref[...]).astype(BF16)

    row = lax.broadcasted_iota(jnp.int32, (GM_CHUNK, GM_CHUNK), 0)
    col = lax.broadcasted_iota(jnp.int32, (GM_CHUNK, GM_CHUNK), 1)
    causal = row >= col

    for g in range(GM_GROUPS):
        lo, hi = g * GM_GROUP_DIM, (g + 1) * GM_GROUP_DIM
        u = _gelu_tanh(_dot(h, win_ref[:, lo:hi]))
        w_s = jnp.where(causal, ws_ref[g], 0.0).astype(BF16)
        bias = bs_ref[:, lo:hi]
        for c in range(tm // GM_CHUNK):
            rows = slice(c * GM_CHUNK, (c + 1) * GM_CHUNK)
            sv = _dot(w_s, v_scr[rows, lo:hi]) + bias
            y_scr[rows, lo:hi] = (u[rows] * sv).astype(BF16)
    o_ref[0] = x + (1.0 + mod[2:3]) * _dot(y_scr[...], wout_ref[...])


def _gmlp_mixer(x, mod, norm_g, layer, w_in, ln_g, ln_b, w_s, b_s, w_out):
    batch, seq, d = x.shape
    tm = TM_GMLP
    bias = jnp.repeat(b_s.T, GM_GROUP_DIM, axis=1)
    const = lambda b, s: (0, 0)
    return pl.pallas_call(
        _gmlp_kernel,
        out_shape=jax.ShapeDtypeStruct(x.shape, F32),
        grid=(batch, seq // tm),
        in_specs=[
            pl.BlockSpec((1, tm, d), lambda b, s: (b, s, 0)),
            pl.BlockSpec((1, 6, d), lambda b, s: (b, 0, 0)),
            pl.BlockSpec((1, d), const),
            _resident(w_in, layer),
            pl.BlockSpec((1, GM_INNER), const),
            pl.BlockSpec((1, GM_INNER), const),
            _resident(w_s, layer),
            _resident(bias),
            _resident(w_out, layer),
        ],
        out_specs=pl.BlockSpec((1, tm, d), lambda b, s: (b, s, 0)),
        scratch_shapes=[pltpu.VMEM((tm, GM_INNER), BF16), pltpu.VMEM((tm, GM_INNER), BF16)],
        compiler_params=_params(("parallel", "parallel")),
        name="gmlp_mixer",
    )(x, mod, norm_g.reshape(1, d), w_in, ln_g.reshape(1, -1), ln_b.reshape(1, -1), w_s, bias, w_out)


def _mla_proj_kernel(x_ref, pos_ref, mod_ref, ng_ref, invf_ref, spread_ref, wa_ref, qg_ref, kvg_ref, wq_ref, wqs_ref,
                     wk_ref, wv_ref, q_ref, k_ref, v_ref, cos_scr, sin_scr, *, scale):
    mod = mod_ref[0]
    h = _rms_mod(x_ref[0], ng_ref[...], mod[0:1], mod[1:2]).astype(BF16)
    a = _dot(h, wa_ref[...])
    cq = _rms(a[:, :MLA_Q_RANK], qg_ref[...]).astype(BF16)
    kv_lo = MLA_Q_RANK
    ckv = _rms(a[:, kv_lo:kv_lo + MLA_KV_RANK], kvg_ref[...]).astype(BF16)
    kr_lo = kv_lo + MLA_KV_RANK
    k_rope, k_rope_sw = a[:, kr_lo:kr_lo + LANES], a[:, kr_lo + LANES:kr_lo + 2 * LANES]

    ang = pos_ref[0] * invf_ref[...]
    n = ang.shape[0]

    def spread(table, scr):
        hi = table.astype(BF16)
        rest = table - hi.astype(F32)
        mid = rest.astype(BF16)
        lo = (rest - mid.astype(F32)).astype(BF16)
        y = _dot(jnp.concatenate([hi, mid, lo], axis=0), spread_ref[...])
        z = y[0:n] + y[n:2 * n] + y[2 * n:3 * n]
        for j in range(ROPE_PACK):
            scr[pl.ds(j, n, stride=ROPE_PACK), :] = z[:, j * LANES:(j + 1) * LANES]
        return scr[...]

    cos, sin = spread(jnp.cos(ang), cos_scr), spread(jnp.sin(ang), sin_scr)

    k_r = (k_rope * cos + k_rope_sw * sin).astype(BF16)
    for hp in range(MLA_HEADS // 2):
        q_main = _dot(cq, wq_ref[:, 2 * hp * MLA_HEAD_PAD:2 * (hp + 1) * MLA_HEAD_PAD])
        q_sw = _dot(cq, wqs_ref[:, 2 * hp * LANES:2 * (hp + 1) * LANES])
        k_nope = _dot(ckv, wk_ref[:, 2 * hp * MLA_NOPE:2 * (hp + 1) * MLA_NOPE])
        v_ref[0, :, 2 * hp * MLA_V:2 * (hp + 1) * MLA_V] = _dot(
            ckv, wv_ref[:, 2 * hp * MLA_V:2 * (hp + 1) * MLA_V]).astype(BF16)
        for i in range(2):
            lo = (2 * hp + i) * MLA_HEAD_PAD
            mid = lo + MLA_NOPE
            src = i * MLA_HEAD_PAD
            q_ref[0, :, lo:mid] = (q_main[:, src:src + MLA_NOPE] * scale).astype(BF16)
            q_r = (q_main[:, src + MLA_NOPE:src + MLA_HEAD_PAD] * cos
                   + q_sw[:, i * LANES:(i + 1) * LANES] * sin)
            q_ref[0, :, mid:mid + LANES] = (q_r * scale).astype(BF16)
            k_ref[0, :, lo:mid] = k_nope[:, i * MLA_NOPE:(i + 1) * MLA_NOPE].astype(BF16)
            k_ref[0, :, mid:mid + LANES] = k_r


def _flash_kernel(q_ref, k_ref, v_ref, o_ref, m_scr, l_scr, acc_scr):
    qi = pl.program_id(2)
    t = q_ref.shape[1]

    m_scr[...] = jnp.full_like(m_scr, -jnp.inf)
    l_scr[...] = jnp.zeros_like(l_scr)
    acc_scr[...] = jnp.zeros_like(acc_scr)

    def block(first, n_keys, masked):
        rows = pl.ds(pl.multiple_of(first * t, t), n_keys)
        for hd in range(FLASH_HEADS):
            qk_lanes = slice(hd * MLA_HEAD_PAD, (hd + 1) * MLA_HEAD_PAD)
            v_lanes = slice(hd * MLA_V, (hd + 1) * MLA_V)
            s = _dot_nt(q_ref[0, :, qk_lanes], k_ref[0, rows, qk_lanes])
            if masked:
                row = lax.broadcasted_iota(jnp.int32, (t, n_keys), 0)
                col = lax.broadcasted_iota(jnp.int32, (t, n_keys), 1)
                s = jnp.where(row >= col, s, -jnp.inf)
            m_prev = m_scr[hd]
            m_next = jnp.maximum(m_prev, jnp.max(s, axis=1, keepdims=True))
            p = jnp.exp2(s - jnp.concatenate([m_next] * (n_keys // LANES), axis=1))
            alpha = jnp.exp2(m_prev - m_next)
            l_scr[hd] = alpha * l_scr[hd] + jnp.sum(p, axis=1, keepdims=True)
            acc_scr[hd] = alpha * acc_scr[hd] + _dot(p.astype(BF16), v_ref[0, rows, v_lanes])
            m_scr[hd] = m_next

    wide = FLASH_WIDE

    def unmasked(jj, carry):
        block(wide * jj, wide * t, False)
        return carry

    lax.fori_loop(0, qi // wide, unmasked, 0)
    done = (qi // wide) * wide
    width = wide // 2
    while width >= 1:
        take = ((qi - done) // width) == 1

        @pl.when(take)
        def _(done=done, width=width):
            block(done, width * t, False)

        done = done + jnp.where(take, width, 0)
        width //= 2

    block(qi, t, True)
    for hd in range(FLASH_HEADS):
        o_ref[0, :, hd * MLA_V:(hd + 1) * MLA_V] = (acc_scr[hd] / l_scr[hd]).astype(BF16)


def _mla_mixer(x, positions, mod, norm_g, w_a, q_norm_g, kv_norm_g, w_qb, w_kvb, w_o):
    batch, seq, d = x.shape
    heads, rope, half = MLA_HEADS, MLA_ROPE, MLA_ROPE // 2
    pad = LANES - rope

    def swap_halves(w):
        return jnp.concatenate([-w[..., half:], w[..., :half]], axis=-1)

    def pad_lanes(w):
        return jnp.pad(w, [(0, 0)] * (w.ndim - 1) + [(0, pad)])

    kr_lo = MLA_Q_RANK + MLA_KV_RANK
    w_kr = w_a[:, kr_lo:]
    w_a_ext = jnp.concatenate([w_a[:, :kr_lo], pad_lanes(w_kr), pad_lanes(swap_halves(w_kr))], axis=1)

    wq = w_qb.reshape(MLA_Q_RANK, heads, MLA_NOPE + rope)
    wq_rope = wq[:, :, MLA_NOPE:]
    w_q_main = jnp.concatenate([wq[:, :, :MLA_NOPE], pad_lanes(wq_rope)], axis=-1).reshape(MLA_Q_RANK, -1)
    w_q_sw = pad_lanes(swap_halves(wq_rope)).reshape(MLA_Q_RANK, -1)

    wkv = w_kvb.reshape(MLA_KV_RANK, heads, MLA_NOPE + MLA_V)
    w_k = wkv[:, :, :MLA_NOPE].reshape(MLA_KV_RANK, -1)
    w_v = wkv[:, :, MLA_NOPE:].reshape(MLA_KV_RANK, -1)

    inv_freq = ROPE_THETA ** (-jnp.arange(0, rope, 2, dtype=F32) / rope)
    inv_freq = jnp.tile(inv_freq, ROPE_PACK).reshape(1, LANES)
    pos = jnp.repeat(positions.astype(F32).reshape(batch, seq // ROPE_PACK, ROPE_PACK), half, axis=2)
    lane = jnp.arange(LANES)
    spread = jnp.concatenate([lane[:, None] == half * j + lane[None, :] % half for j in range(ROPE_PACK)],
                             axis=1).astype(BF16)

    tm = TM_PROJ
    qk_w = heads * MLA_HEAD_PAD
    const = lambda b, s: (0, 0)
    full = lambda arr: pl.BlockSpec(arr.shape, const)
    weights = [w_a_ext.astype(BF16), q_norm_g.reshape(1, -1), kv_norm_g.reshape(1, -1), w_q_main.astype(BF16),
               w_q_sw.astype(BF16), w_k.astype(BF16), w_v.astype(BF16)]
    q, k, v = pl.pallas_call(
        functools.partial(_mla_proj_kernel, scale=LOG2E * (MLA_NOPE + rope) ** -0.5),
        out_shape=(jax.ShapeDtypeStruct((batch, seq, qk_w), BF16),
                   jax.ShapeDtypeStruct((batch, seq, qk_w), BF16),
                   jax.ShapeDtypeStruct((batch, seq, heads * MLA_V), BF16)),
        grid=(batch, seq // tm),
        in_specs=[
            pl.BlockSpec((1, tm, d), lambda b, s: (b, s, 0)),
            pl.BlockSpec((1, tm // ROPE_PACK, LANES), lambda b, s: (b, s, 0)),
            pl.BlockSpec((1, 6, d), lambda b, s: (b, 0, 0)),
            pl.BlockSpec((1, d), const),
            pl.BlockSpec((1, LANES), const),
            full(spread),
        ] + [full(w) for w in weights],
        out_specs=(pl.BlockSpec((1, tm, qk_w), lambda b, s: (b, s, 0)),
                   pl.BlockSpec((1, tm, qk_w), lambda b, s: (b, s, 0)),
                   pl.BlockSpec((1, tm, heads * MLA_V), lambda b, s: (b, s, 0))),
        scratch_shapes=[pltpu.VMEM((tm, LANES), F32), pltpu.VMEM((tm, LANES), F32)],
        compiler_params=_params(("parallel", "parallel")),
        name="mla_proj",
    )(x, pos, mod, norm_g.reshape(1, d), inv_freq, spread, *weights)

    t = TQ_ATTN
    fh = FLASH_HEADS
    o = pl.pallas_call(
        _flash_kernel,
        out_shape=jax.ShapeDtypeStruct((batch, seq, heads * MLA_V), BF16),
        grid=(batch, heads // fh, seq // t),
        in_specs=[
            pl.BlockSpec((1, t, fh * MLA_HEAD_PAD), lambda b, h, i: (b, i, h)),
            pl.BlockSpec((1, seq, fh * MLA_HEAD_PAD), lambda b, h, i: (b, 0, h)),
            pl.BlockSpec((1, seq, fh * MLA_V), lambda b, h, i: (b, 0, h)),
        ],
        out_specs=pl.BlockSpec((1, t, fh * MLA_V), lambda b, h, i: (b, i, h)),
        scratch_shapes=[pltpu.VMEM((fh, t, LANES), F32), pltpu.VMEM((fh, t, LANES), F32),
                        pltpu.VMEM((fh, t, MLA_V), F32)],
        compiler_params=_params(("parallel", "parallel", "arbitrary")),
        name="mla_flash",
    )(q, k, v)
    return o, w_o.astype(BF16)


def _hgrn_proj_kernel(x_ref, mod_ref, ng_ref, lbp_ref, w_ref, q_ref, k_ref, lf_ref, v_ref, sg_ref, *, layer):
    mod = mod_ref[0]
    h = _rms_mod(x_ref[0], ng_ref[...], mod[0:1], mod[1:2]).astype(BF16)
    nk = HG_HEADS * HG_DK
    nv = HG_HEADS * HG_DV

    lbp = lbp_ref[...]
    e = jnp.exp(lbp - jnp.max(lbp, axis=0, keepdims=True))
    p = e / jnp.sum(e, axis=0, keepdims=True)
    lb = jnp.sum(p[1:layer + 1], axis=0, keepdims=True) if layer >= 1 else jnp.zeros((1, nk), F32)

    def store_heads(ref, val, hd0):
        for i in range(val.shape[1] // HG_DK):
            ref[0, hd0 + i] = val[:, i * HG_DK:(i + 1) * HG_DK]

    pair = 2 * HG_DK
    for hp in range(HG_HEADS // 2):
        cols = lambda seg: slice(seg * nk + hp * pair, seg * nk + (hp + 1) * pair)
        store_heads(q_ref, _dot(h, w_ref[:, cols(0)]).astype(BF16), 2 * hp)
        lb_p = lb[:, hp * pair:(hp + 1) * pair]
        f = lb_p + (1.0 - lb_p) * jax.nn.sigmoid(_dot(h, w_ref[:, cols(1)]))
        store_heads(lf_ref, jnp.log(f), 2 * hp)
        store_heads(k_ref, (1.0 - f).astype(BF16), 2 * hp)
        store_heads(v_ref, _dot(h, w_ref[:, cols(2)]).astype(BF16), 2 * hp)
        store_heads(sg_ref, _silu(_dot(h, w_ref[:, cols(3)])).astype(BF16), 2 * hp)


def _hgrn_chunk(q, kk, g, v, state_t, b_scr):
    c_rows = q.shape[0]
    row = lax.broadcasted_iota(jnp.int32, (c_rows, HG_DK), 0)
    trow = lax.broadcasted_iota(jnp.int32, (c_rows, c_rows), 0)
    tcol = lax.broadcasted_iota(jnp.int32, (c_rows, c_rows), 1)

    b = g
    sh = 1
    while sh < c_rows:
        b = b + jnp.where(row >= sh, pltpu.roll(b, sh, 0), 0.0)
        sh *= 2
    b_scr[...] = b

    attn = jnp.where(trow == tcol, _dot_nt(q.astype(BF16), kk.astype(BF16)), 0.0)
    half = 1
    while half < c_rows:
        right = (row & half) != 0
        if half == 1:
            arg = jnp.where(right, g, 0.0)
        elif half == 2:
            phase = row & 3
            arg = jnp.where(phase == 0, pltpu.roll(g, c_rows - 1, 0),
                            jnp.where(phase == 1, 0.0,
                                      jnp.where(phase == 2, g, g + pltpu.roll(g, 1, 0))))
        else:
            blk = 2 * half
            ref_rows = jnp.concatenate(
                [jnp.broadcast_to(b_scr[m * blk + half - 1:m * blk + half, :], (blk, HG_DK))
                 for m in range(c_rows // blk)], axis=0)
            diff = b - ref_rows
            arg = jnp.where(right, diff, -diff)
        x = jnp.exp(arg)
        q_l = jnp.where(right, q * x, 0.0).astype(BF16)
        k_l = jnp.where(right, 0.0, kk * x).astype(BF16)
        same_block = (trow & -(2 * half)) == (tcol & -(2 * half))
        attn = attn + jnp.where(same_block, _dot_nt(q_l, k_l), 0.0)
        half *= 2

    b_end = b_scr[c_rows - 1:c_rows, :]
    o = _dot(attn.astype(BF16), v)
    o = o + _dot_nt((q * jnp.exp(b)).astype(BF16), state_t.astype(BF16))
    k_dec = (kk * jnp.exp(b_end - b)).astype(BF16)
    new_state_t = jnp.exp(b_end) * state_t + _dot_tn(v, k_dec)
    return o, new_state_t


def _hgrn_rec_kernel(q_ref, k_ref, lf_ref, v_ref, sg_ref, g_ref, o_ref, state_scr, b_scr):
    @pl.when(pl.program_id(1) == 0)
    def _():
        state_scr[...] = jnp.zeros_like(state_scr)

    norm_g = g_ref[...]
    tb = q_ref.shape[2]

    n_chunks = tb // HG_KCHUNK

    def head_pair(hp, carry):
        for i in range(2):
            hd = 2 * hp + i
            state_t = state_scr[hd]
            for c in range(n_chunks):
                rows = pl.ds(c * HG_KCHUNK, HG_KCHUNK)
                o, state_t = _hgrn_chunk(q_ref[0, hd, rows, :].astype(F32), k_ref[0, hd, rows, :].astype(F32),
                                         lf_ref[0, hd, rows, :], v_ref[0, hd, rows, :], state_t,
                                         b_scr.at[i * n_chunks + c])
                o_ref[0, hd, rows, :] = (_rms(o, norm_g) * sg_ref[0, hd, rows, :].astype(F32)).astype(BF16)
            state_scr[hd] = state_t
        return carry

    lax.fori_loop(0, HG_HEADS // 2, head_pair, 0)


def _hgrn_mixer(x, mod, norm_g, hg_lb, layer, w_in, out_norm_g, w_o):
    batch, seq, d = x.shape
    nk = HG_HEADS * HG_DK
    nv = HG_HEADS * HG_DV
    tm = TM_PROJ
    const = lambda b, s: (0, 0)
    heads_shape = lambda w, dt: jax.ShapeDtypeStruct((batch, HG_HEADS, seq, w), dt)
    heads_blk = lambda rows, w: pl.BlockSpec((1, HG_HEADS, rows, w), lambda b, s: (b, 0, s, 0))
    q, k, lf, v, sg = pl.pallas_call(
        functools.partial(_hgrn_proj_kernel, layer=layer),
        out_shape=(heads_shape(HG_DK, BF16), heads_shape(HG_DK, BF16), heads_shape(HG_DK, F32),
                   heads_shape(HG_DV, BF16), heads_shape(HG_DV, BF16)),
        grid=(batch, seq // tm),
        in_specs=[
            pl.BlockSpec((1, tm, d), lambda b, s: (b, s, 0)),
            pl.BlockSpec((1, 6, d), lambda b, s: (b, 0, 0)),
            pl.BlockSpec((1, d), const),
            pl.BlockSpec((DEPTH, nk), const),
            pl.BlockSpec(w_in.shape, const),
        ],
        out_specs=(heads_blk(tm, HG_DK), heads_blk(tm, HG_DK), heads_blk(tm, HG_DK), heads_blk(tm, HG_DV),
                   heads_blk(tm, HG_DV)),
        compiler_params=_params(("parallel", "parallel")),
        name="hgrn_proj",
    )(x, mod, norm_g.reshape(1, d), hg_lb, w_in)

    tb = TB_HGRN
    o = pl.pallas_call(
        _hgrn_rec_kernel,
        out_shape=heads_shape(HG_DV, BF16),
        grid=(batch, seq // tb),
        in_specs=[heads_blk(tb, HG_DK), heads_blk(tb, HG_DK), heads_blk(tb, HG_DK), heads_blk(tb, HG_DV),
                  heads_blk(tb, HG_DV), pl.BlockSpec((1, HG_DV), const)],
        out_specs=heads_blk(tb, HG_DV),
        scratch_shapes=[pltpu.VMEM((HG_HEADS, HG_DV, HG_DK), F32),
                        pltpu.VMEM((2 * tb // HG_KCHUNK, HG_KCHUNK, HG_DK), F32)],
        compiler_params=_params(("parallel", "arbitrary")),
        name="hgrn_recurrence",
    )(q, k, lf, v, sg, out_norm_g.reshape(1, HG_DV))
    return o, w_o


def kernel(x, c, positions, ada_w, ada_b, mix_norm_g, ffn_norm_g, gm_w_in, gm_ln_g, gm_ln_b, gm_w_s, gm_b_s,
           gm_w_out, mla_w_a, mla_q_norm_g, mla_kv_norm_g, mla_w_qb, mla_w_kvb, mla_w_o, hg_lb, hg_w_in,
           hg_norm_g, hg_w_o, ff_w_up, ff_conv_w, ff_conv_b, ff_w_down, final_g):
    mod_all = _adaln(c, ada_w, ada_b)
    gm_w_in, gm_w_out = gm_w_in.astype(BF16), gm_w_out.astype(BF16)
    ff_w_up, ff_w_down = ff_w_up.astype(BF16), ff_w_down.astype(BF16)
    for i in range(DEPTH):
        mod = mod_all[i]
        kind, j = i % N_MIXERS, i // N_MIXERS
        proj = None
        if kind == 0:
            x = _gmlp_mixer(x, mod, mix_norm_g[i], j, gm_w_in, gm_ln_g[j], gm_ln_b[j], gm_w_s, gm_b_s[j], gm_w_out)
        elif kind == 1:
            proj = _mla_mixer(x, positions, mod, mix_norm_g[i], mla_w_a[j], mla_q_norm_g[j], mla_kv_norm_g[j],
                              mla_w_qb[j], mla_w_kvb[j], mla_w_o[j])
        else:
            proj = _hgrn_mixer(x, mod, mix_norm_g[i], hg_lb, i, hg_w_in[j].astype(BF16), hg_norm_g[j],
                               hg_w_o[j].astype(BF16))
        x = _conv_ffn(x, mod, ffn_norm_g[i], i, ff_w_up, ff_conv_w, ff_conv_b, ff_w_down, final_g,
                      final_norm=(i == DEPTH - 1), proj=proj)
    return x
```

```python
import functools

import jax
import jax.numpy as jnp
from jax import lax
from jax.experimental import pallas as pl
from jax.experimental.pallas import tpu as pltpu

F32 = jnp.float32
BF16 = jnp.bfloat16

D_MODEL = 1024
DEPTH = 4
N_MIXERS = 3
EPS = 1e-6

GM_CHUNK = 128
GM_GROUPS = 8
GM_INNER = 2 * D_MODEL
GM_GROUP_DIM = GM_INNER // GM_GROUPS

MLA_HEADS = 8
MLA_Q_RANK = D_MODEL // 4
MLA_KV_RANK = D_MODEL // 8
MLA_NOPE = 128
MLA_ROPE = 64
MLA_V = 128
ROPE_THETA = 10000.0

HG_HEADS = 8
HG_DK = 128
HG_DV = D_MODEL // HG_HEADS

FF_DIM = 2816
CONV_W = 3

LOG2E = 1.4426950408889634
LANES = 128
CONV_TAIL = 8
VMEM_LIMIT = 56 * 1024 * 1024
ROPE_PACK = LANES // (MLA_ROPE // 2)
MLA_HEAD_PAD = MLA_NOPE + LANES

TM_FFN = 512
FC_FFN = 256
TM_GMLP = 512
TM_PROJ = 512
TQ_ATTN = 512
FLASH_WIDE = 4
FLASH_HEADS = 4
TB_HGRN = 1024
HG_KCHUNK = 128


def _params(sem):
    return pltpu.CompilerParams(dimension_semantics=sem, vmem_limit_bytes=VMEM_LIMIT)


def _resident(arr, layer=None):
    if layer is None:
        shape, index = arr.shape, (0,) * arr.ndim
    else:
        shape, index = (None,) + arr.shape[1:], (layer,) + (0,) * (arr.ndim - 1)
    return pl.BlockSpec(shape, lambda *_: index, pipeline_mode=pl.Buffered(1))


def _rms(x, g):
    return x * lax.rsqrt(jnp.mean(x * x, axis=-1, keepdims=True) + EPS) * g


def _rms_mod(x, g, shift, scale):
    return _rms(x, g) * (1.0 + scale) + shift


def _dot(a, b):
    return jnp.dot(a, b, preferred_element_type=F32)


def _dot_nt(a, b):
    return lax.dot_general(a, b, (((1,), (1,)), ((), ())), preferred_element_type=F32)


def _dot_tn(a, b):
    return lax.dot_general(a, b, (((0,), (0,)), ((), ())), preferred_element_type=F32)


def _silu(x):
    return x / (1.0 + jnp.exp2(x * (-LOG2E)))


def _gelu_tanh(x):
    c1 = -2.0 * LOG2E * 0.7978845608028654
    return x / (1.0 + jnp.exp2(x * (c1 + (c1 * 0.044715) * (x * x))))


def _adaln_kernel(c_ref, w_ref, b_ref, o_ref):
    c_act = _silu(c_ref[...]).astype(BF16)
    o_ref[0] = _dot(c_act, w_ref[0].astype(BF16)) + b_ref[0]


def _adaln(c, ada_w, ada_b):
    batch = c.shape[0]
    rows = 8
    tn = 1536
    c_pad = jnp.pad(c, ((0, rows - batch), (0, 0)))
    n = 6 * D_MODEL
    mod = pl.pallas_call(
        _adaln_kernel,
        out_shape=jax.ShapeDtypeStruct((DEPTH, rows, n), F32),
        grid=(DEPTH, n // tn),
        in_specs=[
            pl.BlockSpec((rows, D_MODEL), lambda i, j: (0, 0)),
            pl.BlockSpec((1, D_MODEL, tn), lambda i, j: (i, 0, j)),
            pl.BlockSpec((1, 1, tn), lambda i, j: (i, 0, j)),
        ],
        out_specs=pl.BlockSpec((1, rows, tn), lambda i, j: (i, 0, j)),
        compiler_params=_params(("parallel", "parallel")),
        name="adaln_mod",
    )(c_pad, ada_w, ada_b.reshape(DEPTH, 1, n))
    return mod[:, :batch].reshape(DEPTH, batch, 6, D_MODEL)


def _ffn_kernel(*refs, final_norm, fused_proj):
    if fused_proj:
        a_ref, wo_ref, *refs = refs
    x_ref, mod_ref, ng_ref, wu_ref, cw_ref, cb_ref, wd_ref, fg_ref, o_ref, h_scr, act_scr, tail_scr, x_scr = refs
    mod = mod_ref[0]
    tm = x_ref.shape[1]
    tail = tail_scr.shape[1]
    n_chunks = FF_DIM // FC_FFN

    x = x_ref[0]
    if fused_proj:
        if len(a_ref.shape) == 4:
            a = jnp.concatenate([a_ref[0, hd] for hd in range(a_ref.shape[1])], axis=-1)
        else:
            a = a_ref[0]
        x = x + (1.0 + mod[2:3]) * _dot(a, wo_ref[...])
        x_scr[...] = x
    h_scr[...] = _rms_mod(x, ng_ref[...], mod[3:4], mod[4:5]).astype(BF16)
    h = h_scr[...]

    @pl.when(pl.program_id(1) == 0)
    def _():
        tail_scr[...] = jnp.zeros_like(tail_scr)

    def conv(ci):
        lo = ci * FC_FFN
        a = _dot(h, wu_ref[:, lo:lo + FC_FFN])
        ext = jnp.concatenate([tail_scr[ci], a], axis=0)
        tail_scr[ci] = a[tm - tail:]
        cw = cw_ref[:, lo:lo + FC_FFN]
        return (cb_ref[:, lo:lo + FC_FFN] + cw[2:3] * a + cw[1:2] * pltpu.roll(ext, 1, 0)[tail:]
                + cw[0:1] * pltpu.roll(ext, 2, 0)[tail:])

    for c in range(n_chunks):
        act_scr[:, c * FC_FFN:(c + 1) * FC_FFN] = (_silu(conv(c)) * conv(n_chunks + c)).astype(BF16)

    x = x_scr[...] if fused_proj else x_ref[0]
    out = x + (1.0 + mod[5:6]) * _dot(act_scr[...], wd_ref[...])
    if final_norm:
        out = _rms(out, fg_ref[...])
    o_ref[0] = out


def _conv_ffn(x, mod, norm_g, layer, w_up, conv_w, conv_b, w_down, final_g, final_norm, proj=None):
    batch, seq, d = x.shape
    tm = TM_FFN
    const = lambda b, s: (0, 0)
    conv_b = conv_b.reshape(DEPTH, 1, -1)
    tok = pl.BlockSpec((1, tm, d), lambda b, s: (b, s, 0))
    operands, in_specs = [], []
    if proj is not None:
        a, w_o = proj
        if a.ndim == 4:
            in_specs.append(pl.BlockSpec((1, a.shape[1], tm, a.shape[3]), lambda b, s: (b, 0, s, 0)))
        else:
            in_specs.append(pl.BlockSpec((1, tm, a.shape[2]), lambda b, s: (b, s, 0)))
        in_specs.append(_resident(w_o))
        operands += [a, w_o]
    operands += [x, mod, norm_g.reshape(1, d), w_up, conv_w, conv_b, w_down, final_g.reshape(1, d)]
    in_specs += [tok, pl.BlockSpec((1, 6, d), lambda b, s: (b, 0, 0)), pl.BlockSpec((1, d), const),
                 _resident(w_up, layer), _resident(conv_w, layer), _resident(conv_b, layer),
                 _resident(w_down, layer), pl.BlockSpec((1, d), const)]
    return pl.pallas_call(
        functools.partial(_ffn_kernel, final_norm=final_norm, fused_proj=proj is not None),
        out_shape=jax.ShapeDtypeStruct(x.shape, F32),
        grid=(batch, seq // tm),
        in_specs=in_specs,
        out_specs=tok,
        scratch_shapes=[pltpu.VMEM((tm, d), BF16), pltpu.VMEM((tm, FF_DIM), BF16),
                        pltpu.VMEM((2 * FF_DIM // FC_FFN, CONV_TAIL, FC_FFN), F32),
                        pltpu.VMEM((tm, d) if proj is not None else (8, LANES), F32)],
        compiler_params=_params(("parallel", "arbitrary")),
        name="conv_ffn",
    )(*operands)


def _gmlp_kernel(x_ref, mod_ref, ng_ref, win_ref, lng_ref, lnb_ref, ws_ref, bs_ref, wout_ref, o_ref, v_scr,
                 y_scr):
    mod = mod_ref[0]
    x = x_ref[0]
    tm = x.shape[0]
    h = _rms_mod(x, ng_ref[...], mod[0:1], mod[1:2]).astype(BF16)

    v = _gelu_tanh(_dot(h, win_ref[:, GM_INNER:]))
    mu = jnp.mean(v, axis=-1, keepdims=True)
    vc = v - mu
    var = jnp.mean(vc * vc, axis=-1, keepdims=True)
    v_scr[...] = (vc * lax.rsqrt(var + EPS) * lng_ref[...] + lnb_ref[...]).astype(BF16)

    row = lax.broadcasted_iota(jnp.int32, (GM_CHUNK, GM_CHUNK), 0)
    col = lax.broadcasted_iota(jnp.int32, (GM_CHUNK, GM_CHUNK), 1)
    causal = row >= col

    for g in range(GM_GROUPS):
        lo, hi = g * GM_GROUP_DIM, (g + 1) * GM_GROUP_DIM
        u = _gelu_tanh(_dot(h, win_ref[:, lo:hi]))
        w_s = jnp.where(causal, ws_ref[g], 0.0).astype(BF16)
        bias = bs_ref[:, lo:hi]
        for c in range(tm // GM_CHUNK):
            rows = slice(c * GM_CHUNK, (c + 1) * GM_CHUNK)
            sv = _dot(w_s, v_scr[rows, lo:hi]) + bias
            y_scr[rows, lo:hi] = (u[rows] * sv).astype(BF16)
    o_ref[0] = x + (1.0 + mod[2:3]) * _dot(y_scr[...], wout_ref[...])


def _gmlp_mixer(x, mod, norm_g, layer, w_in, ln_g, ln_b, w_s, b_s, w_out):
    batch, seq, d = x.shape
    tm = TM_GMLP
    bias = jnp.repeat(b_s.T, GM_GROUP_DIM, axis=1)
    const = lambda b, s: (0, 0)
    return pl.pallas_call(
        _gmlp_kernel,
        out_shape=jax.ShapeDtypeStruct(x.shape, F32),
        grid=(batch, seq // tm),
        in_specs=[
            pl.BlockSpec((1, tm, d), lambda b, s: (b, s, 0)),
            pl.BlockSpec((1, 6, d), lambda b, s: (b, 0, 0)),
            pl.BlockSpec((1, d), const),
            _resident(w_in, layer),
            pl.BlockSpec((1, GM_INNER), const),
            pl.BlockSpec((1, GM_INNER), const),
            _resident(w_s, layer),
            _resident(bias),
            _resident(w_out, layer),
        ],
        out_specs=pl.BlockSpec((1, tm, d), lambda b, s: (b, s, 0)),
        scratch_shapes=[pltpu.VMEM((tm, GM_INNER), BF16), pltpu.VMEM((tm, GM_INNER), BF16)],
        compiler_params=_params(("parallel", "parallel")),
        name="gmlp_mixer",
    )(x, mod, norm_g.reshape(1, d), w_in, ln_g.reshape(1, -1), ln_b.reshape(1, -1), w_s, bias, w_out)


def _mla_proj_kernel(x_ref, pos_ref, mod_ref, ng_ref, invf_ref, spread_ref, wa_ref, qg_ref, kvg_ref, wq_ref, wqs_ref,
                     wk_ref, wv_ref, q_ref, k_ref, v_ref, cos_scr, sin_scr, *, scale):
    mod = mod_ref[0]
    h = _rms_mod(x_ref[0], ng_ref[...], mod[0:1], mod[1:2]).astype(BF16)
    a = _dot(h, wa_ref[...])
    cq = _rms(a[:, :MLA_Q_RANK], qg_ref[...]).astype(BF16)
    kv_lo = MLA_Q_RANK
    ckv = _rms(a[:, kv_lo:kv_lo + MLA_KV_RANK], kvg_ref[...]).astype(BF16)
    kr_lo = kv_lo + MLA_KV_RANK
    k_rope, k_rope_sw = a[:, kr_lo:kr_lo + LANES], a[:, kr_lo + LANES:kr_lo + 2 * LANES]

    ang = pos_ref[0] * invf_ref[...]
    n = ang.shape[0]

    def spread(table, scr):
        hi = table.astype(BF16)
        rest = table - hi.astype(F32)
        mid = rest.astype(BF16)
        lo = (rest - mid.astype(F32)).astype(BF16)
        y = _dot(jnp.concatenate([hi, mid, lo], axis=0), spread_ref[...])
        z = y[0:n] + y[n:2 * n] + y[2 * n:3 * n]
        for j in range(ROPE_PACK):
            scr[pl.ds(j, n, stride=ROPE_PACK), :] = z[:, j * LANES:(j + 1) * LANES]
        return scr[...]

    cos, sin = spread(jnp.cos(ang), cos_scr), spread(jnp.sin(ang), sin_scr)

    k_r = (k_rope * cos + k_rope_sw * sin).astype(BF16)
    for hp in range(MLA_HEADS // 2):
        q_main = _dot(cq, wq_ref[:, 2 * hp * MLA_HEAD_PAD:2 * (hp + 1) * MLA_HEAD_PAD])
        q_sw = _dot(cq, wqs_ref[:, 2 * hp * LANES:2 * (hp + 1) * LANES])
        k_nope = _dot(ckv, wk_ref[:, 2 * hp * MLA_NOPE:2 * (hp + 1) * MLA_NOPE])
        v_ref[0, :, 2 * hp * MLA_V:2 * (hp + 1) * MLA_V] = _dot(
            ckv, wv_ref[:, 2 * hp * MLA_V:2 * (hp + 1) * MLA_V]).astype(BF16)
        for i in range(2):
            lo = (2 * hp + i) * MLA_HEAD_PAD
            mid = lo + MLA_NOPE
            src = i * MLA_HEAD_PAD
            q_ref[0, :, lo:mid] = (q_main[:, src:src + MLA_NOPE] * scale).astype(BF16)
            q_r = (q_main[:, src + MLA_NOPE:src + MLA_HEAD_PAD] * cos
                   + q_sw[:, i * LANES:(i + 1) * LANES] * sin)
            q_ref[0, :, mid:mid + LANES] = (q_r * scale).astype(BF16)
            k_ref[0, :, lo:mid] = k_nope[:, i * MLA_NOPE:(i + 1) * MLA_NOPE].astype(BF16)
            k_ref[0, :, mid:mid + LANES] = k_r


def _flash_kernel(q_ref, k_ref, v_ref, o_ref, m_scr, l_scr, acc_scr):
    qi = pl.program_id(2)
    t = q_ref.shape[1]

    m_scr[...] = jnp.full_like(m_scr, -jnp.inf)
    l_scr[...] = jnp.zeros_like(l_scr)
    acc_scr[...] = jnp.zeros_like(acc_scr)

    def block(first, n_keys, ends_on_diagonal):
        rows = pl.ds(pl.multiple_of(first * t, t), n_keys)
        for hd in range(FLASH_HEADS):
            qk_lanes = slice(hd * MLA_HEAD_PAD, (hd + 1) * MLA_HEAD_PAD)
            v_lanes = slice(hd * MLA_V, (hd + 1) * MLA_V)
            s = _dot_nt(q_ref[0, :, qk_lanes], k_ref[0, rows, qk_lanes])
            if ends_on_diagonal:
                row = lax.broadcasted_iota(jnp.int32, (t, n_keys), 0)
                col = lax.broadcasted_iota(jnp.int32, (t, n_keys), 1)
                s = jnp.where(row + (n_keys - t) >= col, s, -jnp.inf)
            m_prev = m_scr[hd]
            m_next = jnp.maximum(m_prev, jnp.max(s, axis=1, keepdims=True))
            p = jnp.exp2(s - jnp.concatenate([m_next] * (n_keys // LANES), axis=1))
            alpha = jnp.exp2(m_prev - m_next)
            l_scr[hd] = alpha * l_scr[hd] + jnp.sum(p, axis=1, keepdims=True)
            acc_scr[hd] = alpha * acc_scr[hd] + _dot(p.astype(BF16), v_ref[0, rows, v_lanes])
            m_scr[hd] = m_next

    wide = FLASH_WIDE

    def unmasked(jj, carry):
        block(wide * jj, wide * t, False)
        return carry

    lax.fori_loop(0, qi // wide, unmasked, 0)
    done = (qi // wide) * wide
    for rest in range(wide):
        @pl.when(qi - done == rest)
        def _(rest=rest):
            block(done, (rest + 1) * t, True)
    for hd in range(FLASH_HEADS):
        o_ref[0, :, hd * MLA_V:(hd + 1) * MLA_V] = (acc_scr[hd] / l_scr[hd]).astype(BF16)


def _mla_mixer(x, positions, mod, norm_g, w_a, q_norm_g, kv_norm_g, w_qb, w_kvb, w_o):
    batch, seq, d = x.shape
    heads, rope, half = MLA_HEADS, MLA_ROPE, MLA_ROPE // 2
    pad = LANES - rope

    def swap_halves(w):
        return jnp.concatenate([-w[..., half:], w[..., :half]], axis=-1)

    def pad_lanes(w):
        return jnp.pad(w, [(0, 0)] * (w.ndim - 1) + [(0, pad)])

    kr_lo = MLA_Q_RANK + MLA_KV_RANK
    w_kr = w_a[:, kr_lo:]
    w_a_ext = jnp.concatenate([w_a[:, :kr_lo], pad_lanes(w_kr), pad_lanes(swap_halves(w_kr))], axis=1)

    wq = w_qb.reshape(MLA_Q_RANK, heads, MLA_NOPE + rope)
    wq_rope = wq[:, :, MLA_NOPE:]
    w_q_main = jnp.concatenate([wq[:, :, :MLA_NOPE], pad_lanes(wq_rope)], axis=-1).reshape(MLA_Q_RANK, -1)
    w_q_sw = pad_lanes(swap_halves(wq_rope)).reshape(MLA_Q_RANK, -1)

    wkv = w_kvb.reshape(MLA_KV_RANK, heads, MLA_NOPE + MLA_V)
    w_k = wkv[:, :, :MLA_NOPE].reshape(MLA_KV_RANK, -1)
    w_v = wkv[:, :, MLA_NOPE:].reshape(MLA_KV_RANK, -1)

    inv_freq = ROPE_THETA ** (-jnp.arange(0, rope, 2, dtype=F32) / rope)
    inv_freq = jnp.tile(inv_freq, ROPE_PACK).reshape(1, LANES)
    pos = jnp.repeat(positions.astype(F32).reshape(batch, seq // ROPE_PACK, ROPE_PACK), half, axis=2)
    lane = jnp.arange(LANES)
    spread = jnp.concatenate([lane[:, None] == half * j + lane[None, :] % half for j in range(ROPE_PACK)],
                             axis=1).astype(BF16)

    tm = TM_PROJ
    qk_w = heads * MLA_HEAD_PAD
    const = lambda b, s: (0, 0)
    full = lambda arr: pl.BlockSpec(arr.shape, const)
    weights = [w_a_ext.astype(BF16), q_norm_g.reshape(1, -1), kv_norm_g.reshape(1, -1), w_q_main.astype(BF16),
               w_q_sw.astype(BF16), w_k.astype(BF16), w_v.astype(BF16)]
    q, k, v = pl.pallas_call(
        functools.partial(_mla_proj_kernel, scale=LOG2E * (MLA_NOPE + rope) ** -0.5),
        out_shape=(jax.ShapeDtypeStruct((batch, seq, qk_w), BF16),
                   jax.ShapeDtypeStruct((batch, seq, qk_w), BF16),
                   jax.ShapeDtypeStruct((batch, seq, heads * MLA_V), BF16)),
        grid=(batch, seq // tm),
        in_specs=[
            pl.BlockSpec((1, tm, d), lambda b, s: (b, s, 0)),
            pl.BlockSpec((1, tm // ROPE_PACK, LANES), lambda b, s: (b, s, 0)),
            pl.BlockSpec((1, 6, d), lambda b, s: (b, 0, 0)),
            pl.BlockSpec((1, d), const),
            pl.BlockSpec((1, LANES), const),
            full(spread),
        ] + [full(w) for w in weights],
        out_specs=(pl.BlockSpec((1, tm, qk_w), lambda b, s: (b, s, 0)),
                   pl.BlockSpec((1, tm, qk_w), lambda b, s: (b, s, 0)),
                   pl.BlockSpec((1, tm, heads * MLA_V), lambda b, s: (b, s, 0))),
        scratch_shapes=[pltpu.VMEM((tm, LANES), F32), pltpu.VMEM((tm, LANES), F32)],
        compiler_params=_params(("parallel", "parallel")),
        name="mla_proj",
    )(x, pos, mod, norm_g.reshape(1, d), inv_freq, spread, *weights)

    t = TQ_ATTN
    fh = FLASH_HEADS
    o = pl.pallas_call(
        _flash_kernel,
        out_shape=jax.ShapeDtypeStruct((batch, seq, heads * MLA_V), BF16),
        grid=(batch, heads // fh, seq // t),
        in_specs=[
            pl.BlockSpec((1, t, fh * MLA_HEAD_PAD), lambda b, h, i: (b, i, h)),
            pl.BlockSpec((1, seq, fh * MLA_HEAD_PAD), lambda b, h, i: (b, 0, h)),
            pl.BlockSpec((1, seq, fh * MLA_V), lambda b, h, i: (b, 0, h)),
        ],
        out_specs=pl.BlockSpec((1, t, fh * MLA_V), lambda b, h, i: (b, i, h)),
        scratch_shapes=[pltpu.VMEM((fh, t, LANES), F32), pltpu.VMEM((fh, t, LANES), F32),
                        pltpu.VMEM((fh, t, MLA_V), F32)],
        compiler_params=_params(("parallel", "parallel", "arbitrary")),
        name="mla_flash",
    )(q, k, v)
    return o, w_o.astype(BF16)


def _hgrn_proj_kernel(x_ref, mod_ref, ng_ref, lbp_ref, w_ref, q_ref, k_ref, lf_ref, v_ref, sg_ref, *, layer):
    mod = mod_ref[0]
    h = _rms_mod(x_ref[0], ng_ref[...], mod[0:1], mod[1:2]).astype(BF16)
    nk = HG_HEADS * HG_DK
    nv = HG_HEADS * HG_DV

    lbp = lbp_ref[...]
    e = jnp.exp(lbp - jnp.max(lbp, axis=0, keepdims=True))
    p = e / jnp.sum(e, axis=0, keepdims=True)
    lb = jnp.sum(p[1:layer + 1], axis=0, keepdims=True) if layer >= 1 else jnp.zeros((1, nk), F32)

    def store_heads(ref, val, hd0):
        for i in range(val.shape[1] // HG_DK):
            ref[0, hd0 + i] = val[:, i * HG_DK:(i + 1) * HG_DK]

    pair = 2 * HG_DK
    for hp in range(HG_HEADS // 2):
        cols = lambda seg: slice(seg * nk + hp * pair, seg * nk + (hp + 1) * pair)
        store_heads(q_ref, _dot(h, w_ref[:, cols(0)]).astype(BF16), 2 * hp)
        lb_p = lb[:, hp * pair:(hp + 1) * pair]
        f = lb_p + (1.0 - lb_p) * jax.nn.sigmoid(_dot(h, w_ref[:, cols(1)]))
        store_heads(lf_ref, jnp.log(f), 2 * hp)
        store_heads(k_ref, (1.0 - f).astype(BF16), 2 * hp)
        store_heads(v_ref, _dot(h, w_ref[:, cols(2)]).astype(BF16), 2 * hp)
        store_heads(sg_ref, _silu(_dot(h, w_ref[:, cols(3)])).astype(BF16), 2 * hp)


def _hgrn_chunk(q, kk, g, v, state_t, b_scr):
    c_rows = q.shape[0]
    row = lax.broadcasted_iota(jnp.int32, (c_rows, HG_DK), 0)
    trow = lax.broadcasted_iota(jnp.int32, (c_rows, c_rows), 0)
    tcol = lax.broadcasted_iota(jnp.int32, (c_rows, c_rows), 1)

    b = g
    sh = 1
    while sh < c_rows:
        b = b + jnp.where(row >= sh, pltpu.roll(b, sh, 0), 0.0)
        sh *= 2
    b_scr[...] = b

    attn = jnp.where(trow == tcol, _dot_nt(q.astype(BF16), kk.astype(BF16)), 0.0)
    half = 1
    while half < c_rows:
        right = (row & half) != 0
        if half == 1:
            arg = jnp.where(right, g, 0.0)
        elif half == 2:
            phase = row & 3
            arg = jnp.where(phase == 0, pltpu.roll(g, c_rows - 1, 0),
                            jnp.where(phase == 1, 0.0,
                                      jnp.where(phase == 2, g, g + pltpu.roll(g, 1, 0))))
        else:
            blk = 2 * half
            ref_rows = jnp.concatenate(
                [jnp.broadcast_to(b_scr[m * blk + half - 1:m * blk + half, :], (blk, HG_DK))
                 for m in range(c_rows // blk)], axis=0)
            diff = b - ref_rows
            arg = jnp.where(right, diff, -diff)
        x = jnp.exp(arg)
        q_l = jnp.where(right, q * x, 0.0).astype(BF16)
        k_l = jnp.where(right, 0.0, kk * x).astype(BF16)
        same_block = (trow & -(2 * half)) == (tcol & -(2 * half))
        attn = attn + jnp.where(same_block, _dot_nt(q_l, k_l), 0.0)
        half *= 2

    b_end = b_scr[c_rows - 1:c_rows, :]
    o = _dot(attn.astype(BF16), v)
    o = o + _dot_nt((q * jnp.exp(b)).astype(BF16), state_t.astype(BF16))
    k_dec = (kk * jnp.exp(b_end - b)).astype(BF16)
    new_state_t = jnp.exp(b_end) * state_t + _dot_tn(v, k_dec)
    return o, new_state_t


def _hgrn_rec_kernel(q_ref, k_ref, lf_ref, v_ref, sg_ref, g_ref, o_ref, state_scr, b_scr):
    @pl.when(pl.program_id(1) == 0)
    def _():
        state_scr[...] = jnp.zeros_like(state_scr)

    norm_g = g_ref[...]
    tb = q_ref.shape[2]

    n_chunks = tb // HG_KCHUNK

    def head_pair(hp, carry):
        for i in range(2):
            hd = 2 * hp + i
            state_t = state_scr[hd]
            for c in range(n_chunks):
                rows = pl.ds(c * HG_KCHUNK, HG_KCHUNK)
                o, state_t = _hgrn_chunk(q_ref[0, hd, rows, :].astype(F32), k_ref[0, hd, rows, :].astype(F32),
                                         lf_ref[0, hd, rows, :], v_ref[0, hd, rows, :], state_t,
                                         b_scr.at[i * n_chunks + c])
                o_ref[0, hd, rows, :] = (_rms(o, norm_g) * sg_ref[0, hd, rows, :].astype(F32)).astype(BF16)
            state_scr[hd] = state_t
        return carry

    lax.fori_loop(0, HG_HEADS // 2, head_pair, 0)


def _hgrn_mixer(x, mod, norm_g, hg_lb, layer, w_in, out_norm_g, w_o):
    batch, seq, d = x.shape
    nk = HG_HEADS * HG_DK
    nv = HG_HEADS * HG_DV
    tm = TM_PROJ
    const = lambda b, s: (0, 0)
    heads_shape = lambda w, dt: jax.ShapeDtypeStruct((batch, HG_HEADS, seq, w), dt)
    heads_blk = lambda rows, w: pl.BlockSpec((1, HG_HEADS, rows, w), lambda b, s: (b, 0, s, 0))
    q, k, lf, v, sg = pl.pallas_call(
        functools.partial(_hgrn_proj_kernel, layer=layer),
        out_shape=(heads_shape(HG_DK, BF16), heads_shape(HG_DK, BF16), heads_shape(HG_DK, F32),
                   heads_shape(HG_DV, BF16), heads_shape(HG_DV, BF16)),
        grid=(batch, seq // tm),
        in_specs=[
            pl.BlockSpec((1, tm, d), lambda b, s: (b, s, 0)),
            pl.BlockSpec((1, 6, d), lambda b, s: (b, 0, 0)),
            pl.BlockSpec((1, d), const),
            pl.BlockSpec((DEPTH, nk), const),
            pl.BlockSpec(w_in.shape, const),
        ],
        out_specs=(heads_blk(tm, HG_DK), heads_blk(tm, HG_DK), heads_blk(tm, HG_DK), heads_blk(tm, HG_DV),
                   heads_blk(tm, HG_DV)),
        compiler_params=_params(("parallel", "parallel")),
        name="hgrn_proj",
    )(x, mod, norm_g.reshape(1, d), hg_lb, w_in)

    tb = TB_HGRN
    o = pl.pallas_call(
        _hgrn_rec_kernel,
        out_shape=heads_shape(HG_DV, BF16),
        grid=(batch, seq // tb),
        in_specs=[heads_blk(tb, HG_DK), heads_blk(tb, HG_DK), heads_blk(tb, HG_DK), heads_blk(tb, HG_DV),
                  heads_blk(tb, HG_DV), pl.BlockSpec((1, HG_DV), const)],
        out_specs=heads_blk(tb, HG_DV),
        scratch_shapes=[pltpu.VMEM((HG_HEADS, HG_DV, HG_DK), F32),
                        pltpu.VMEM((2 * tb // HG_KCHUNK, HG_KCHUNK, HG_DK), F32)],
        compiler_params=_params(("parallel", "arbitrary")),
        name="hgrn_recurrence",
    )(q, k, lf, v, sg, out_norm_g.reshape(1, HG_DV))
    return o, w_o


def kernel(x, c, positions, ada_w, ada_b, mix_norm_g, ffn_norm_g, gm_w_in, gm_ln_g, gm_ln_b, gm_w_s, gm_b_s,
           gm_w_out, mla_w_a, mla_q_norm_g, mla_kv_norm_g, mla_w_qb, mla_w_kvb, mla_w_o, hg_lb, hg_w_in,
           hg_norm_g, hg_w_o, ff_w_up, ff_conv_w, ff_conv_b, ff_w_down, final_g):
    mod_all = _adaln(c, ada_w, ada_b)
    gm_w_in, gm_w_out = gm_w_in.astype(BF16), gm_w_out.astype(BF16)
    ff_w_up, ff_w_down = ff_w_up.astype(BF16), ff_w_down.astype(BF16)
    for i in range(DEPTH):
        mod = mod_all[i]
        kind, j = i % N_MIXERS, i // N_MIXERS
        proj = None
        if kind == 0:
            x = _gmlp_mixer(x, mod, mix_norm_g[i], j, gm_w_in, gm_ln_g[j], gm_ln_b[j], gm_w_s, gm_b_s[j], gm_w_out)
        elif kind == 1:
            proj = _mla_mixer(x, positions, mod, mix_norm_g[i], mla_w_a[j], mla_q_norm_g[j], mla_kv_norm_g[j],
                              mla_w_qb[j], mla_w_kvb[j], mla_w_o[j])
        else:
            proj = _hgrn_mixer(x, mod, mix_norm_g[i], hg_lb, i, hg_w_in[j].astype(BF16), hg_norm_g[j],
                               hg_w_o[j].astype(BF16))
        x = _conv_ffn(x, mod, ffn_norm_g[i], i, ff_w_up, ff_conv_w, ff_conv_b, ff_w_down, final_g,
                      final_norm=(i == DEPTH - 1), proj=proj)
    return x
```

```python
import functools

import jax
import jax.numpy as jnp
from jax import lax
from jax.experimental import pallas as pl
from jax.experimental.pallas import tpu as pltpu

F32 = jnp.float32
BF16 = jnp.bfloat16

D_MODEL = 1024
DEPTH = 4
N_MIXERS = 3
EPS = 1e-6

GM_CHUNK = 128
GM_GROUPS = 8
GM_INNER = 2 * D_MODEL
GM_GROUP_DIM = GM_INNER // GM_GROUPS

MLA_HEADS = 8
MLA_Q_RANK = D_MODEL // 4
MLA_KV_RANK = D_MODEL // 8
MLA_NOPE = 128
MLA_ROPE = 64
MLA_V = 128
ROPE_THETA = 10000.0

HG_HEADS = 8
HG_DK = 128
HG_DV = D_MODEL // HG_HEADS

FF_DIM = 2816
CONV_W = 3

LOG2E = 1.4426950408889634
LANES = 128
CONV_TAIL = 8
VMEM_LIMIT = 56 * 1024 * 1024
ROPE_PACK = LANES // (MLA_ROPE // 2)
MLA_HEAD_PAD = MLA_NOPE + LANES

TM_FFN = 1024
FC_FFN = 256
TM_GMLP = 1024
TM_PROJ = 1024
TQ_ATTN = 512
FLASH_WIDE = 4
FLASH_HEADS = 4
TB_HGRN = 1024
HG_KCHUNK = 128


def _params(sem):
    return pltpu.CompilerParams(dimension_semantics=sem, vmem_limit_bytes=VMEM_LIMIT)


def _resident(arr, layer=None):
    if layer is None:
        shape, index = arr.shape, (0,) * arr.ndim
    else:
        shape, index = (None,) + arr.shape[1:], (layer,) + (0,) * (arr.ndim - 1)
    return pl.BlockSpec(shape, lambda *_: index, pipeline_mode=pl.Buffered(1))


def _rms(x, g):
    return x * lax.rsqrt(jnp.mean(x * x, axis=-1, keepdims=True) + EPS) * g


def _rms_mod(x, g, shift, scale):
    return _rms(x, g * (1.0 + scale)) + shift


def _dot(a, b):
    return jnp.dot(a, b, preferred_element_type=F32)


def _dot_nt(a, b):
    return lax.dot_general(a, b, (((1,), (1,)), ((), ())), preferred_element_type=F32)


def _dot_tn(a, b):
    return lax.dot_general(a, b, (((0,), (0,)), ((), ())), preferred_element_type=F32)


def _silu(x):
    return x / (1.0 + jnp.exp2(x * (-LOG2E)))


def _gelu_tanh(x):
    c1 = -2.0 * LOG2E * 0.7978845608028654
    return x / (1.0 + jnp.exp2(x * (c1 + (c1 * 0.044715) * (x * x))))


def _adaln_kernel(c_ref, w_ref, b_ref, o_ref):
    c_act = _silu(c_ref[...]).astype(BF16)
    o_ref[0] = _dot(c_act, w_ref[0].astype(BF16)) + b_ref[0]


def _adaln(c, ada_w, ada_b):
    batch = c.shape[0]
    rows = 8
    tn = 1536
    c_pad = jnp.pad(c, ((0, rows - batch), (0, 0)))
    n = 6 * D_MODEL
    mod = pl.pallas_call(
        _adaln_kernel,
        out_shape=jax.ShapeDtypeStruct((DEPTH, rows, n), F32),
        grid=(DEPTH, n // tn),
        in_specs=[
            pl.BlockSpec((rows, D_MODEL), lambda i, j: (0, 0)),
            pl.BlockSpec((1, D_MODEL, tn), lambda i, j: (i, 0, j)),
            pl.BlockSpec((1, 1, tn), lambda i, j: (i, 0, j)),
        ],
        out_specs=pl.BlockSpec((1, rows, tn), lambda i, j: (i, 0, j)),
        compiler_params=_params(("parallel", "parallel")),
        name="adaln_mod",
    )(c_pad, ada_w, ada_b.reshape(DEPTH, 1, n))
    return mod[:, :batch].reshape(DEPTH, batch, 6, D_MODEL)


def _ffn_kernel(*refs, final_norm, fused_proj):
    if fused_proj:
        a_ref, wo_ref, *refs = refs
    x_ref, mod_ref, ng_ref, wu_ref, cw_ref, cb_ref, wd_ref, fg_ref, o_ref, h_scr, act_scr, tail_scr, x_scr = refs
    mod = mod_ref[0]
    tm = x_ref.shape[1]
    tail = tail_scr.shape[1]
    n_chunks = FF_DIM // FC_FFN

    x = x_ref[0]
    if fused_proj:
        if len(a_ref.shape) == 4:
            a = jnp.concatenate([a_ref[0, hd] for hd in range(a_ref.shape[1])], axis=-1)
        else:
            a = a_ref[0]
        x = x + (1.0 + mod[2:3]) * _dot(a, wo_ref[...])
        x_scr[...] = x
    h_scr[...] = _rms_mod(x, ng_ref[...], mod[3:4], mod[4:5]).astype(BF16)
    h = h_scr[...]

    @pl.when(pl.program_id(1) == 0)
    def _():
        tail_scr[...] = jnp.zeros_like(tail_scr)

    def conv(ci):
        lo = ci * FC_FFN
        a = _dot(h, wu_ref[:, lo:lo + FC_FFN])
        ext = jnp.concatenate([tail_scr[ci], a], axis=0)
        tail_scr[ci] = a[tm - tail:]
        cw = cw_ref[:, lo:lo + FC_FFN]
        return (cb_ref[:, lo:lo + FC_FFN] + cw[2:3] * a + cw[1:2] * pltpu.roll(ext, 1, 0)[tail:]
                + cw[0:1] * pltpu.roll(ext, 2, 0)[tail:])

    for c in range(n_chunks):
        act_scr[:, c * FC_FFN:(c + 1) * FC_FFN] = (_silu(conv(c)) * conv(n_chunks + c)).astype(BF16)

    x = x_scr[...] if fused_proj else x_ref[0]
    out = x + (1.0 + mod[5:6]) * _dot(act_scr[...], wd_ref[...])
    if final_norm:
        out = _rms(out, fg_ref[...])
    o_ref[0] = out


def _conv_ffn(x, mod, norm_g, layer, w_up, conv_w, conv_b, w_down, final_g, final_norm, proj=None):
    batch, seq, d = x.shape
    tm = TM_FFN
    const = lambda b, s: (0, 0)
    conv_b = conv_b.reshape(DEPTH, 1, -1)
    tok = pl.BlockSpec((1, tm, d), lambda b, s: (b, s, 0))
    operands, in_specs = [], []
    if proj is not None:
        a, w_o = proj
        if a.ndim == 4:
            in_specs.append(pl.BlockSpec((1, a.shape[1], tm, a.shape[3]), lambda b, s: (b, 0, s, 0)))
        else:
            in_specs.append(pl.BlockSpec((1, tm, a.shape[2]), lambda b, s: (b, s, 0)))
        in_specs.append(_resident(w_o))
        operands += [a, w_o]
    operands += [x, mod, norm_g.reshape(1, d), w_up, conv_w, conv_b, w_down, final_g.reshape(1, d)]
    in_specs += [tok, pl.BlockSpec((1, 6, d), lambda b, s: (b, 0, 0)), pl.BlockSpec((1, d), const),
                 _resident(w_up, layer), _resident(conv_w, layer), _resident(conv_b, layer),
                 _resident(w_down, layer), pl.BlockSpec((1, d), const)]
    return pl.pallas_call(
        functools.partial(_ffn_kernel, final_norm=final_norm, fused_proj=proj is not None),
        out_shape=jax.ShapeDtypeStruct(x.shape, F32),
        grid=(batch, seq // tm),
        in_specs=in_specs,
        out_specs=tok,
        scratch_shapes=[pltpu.VMEM((tm, d), BF16), pltpu.VMEM((tm, FF_DIM), BF16),
                        pltpu.VMEM((2 * FF_DIM // FC_FFN, CONV_TAIL, FC_FFN), F32),
                        pltpu.VMEM((tm, d) if proj is not None else (8, LANES), F32)],
        compiler_params=_params(("parallel", "arbitrary")),
        name="conv_ffn",
    )(*operands)


def _gmlp_kernel(x_ref, mod_ref, ng_ref, win_ref, lng_ref, lnb_ref, ws_ref, bs_ref, wout_ref, o_ref, v_scr,
                 y_scr):
    mod = mod_ref[0]
    x = x_ref[0]
    tm = x.shape[0]
    h = _rms_mod(x, ng_ref[...], mod[0:1], mod[1:2]).astype(BF16)

    v = _gelu_tanh(_dot(h, win_ref[:, GM_INNER:]))
    mu = jnp.mean(v, axis=-1, keepdims=True)
    vc = v - mu
    var = jnp.mean(vc * vc, axis=-1, keepdims=True)
    v_scr[...] = (vc * lax.rsqrt(var + EPS) * lng_ref[...] + lnb_ref[...]).astype(BF16)

    row = lax.broadcasted_iota(jnp.int32, (GM_CHUNK, GM_CHUNK), 0)
    col = lax.broadcasted_iota(jnp.int32, (GM_CHUNK, GM_CHUNK), 1)
    causal = row >= col

    for g in range(GM_GROUPS):
        lo, hi = g * GM_GROUP_DIM, (g + 1) * GM_GROUP_DIM
        u = _gelu_tanh(_dot(h, win_ref[:, lo:hi]))
        w_s = jnp.where(causal, ws_ref[g], 0.0).astype(BF16)
        bias = bs_ref[:, lo:hi]
        for c in range(tm // GM_CHUNK):
            rows = slice(c * GM_CHUNK, (c + 1) * GM_CHUNK)
            sv = _dot(w_s, v_scr[rows, lo:hi]) + bias
            y_scr[rows, lo:hi] = (u[rows] * sv).astype(BF16)
    o_ref[0] = x + (1.0 + mod[2:3]) * _dot(y_scr[...], wout_ref[...])


def _gmlp_mixer(x, mod, norm_g, layer, w_in, ln_g, ln_b, w_s, b_s, w_out):
    batch, seq, d = x.shape
    tm = TM_GMLP
    bias = jnp.repeat(b_s.T, GM_GROUP_DIM, axis=1)
    const = lambda b, s: (0, 0)
    return pl.pallas_call(
        _gmlp_kernel,
        out_shape=jax.ShapeDtypeStruct(x.shape, F32),
        grid=(batch, seq // tm),
        in_specs=[
            pl.BlockSpec((1, tm, d), lambda b, s: (b, s, 0)),
            pl.BlockSpec((1, 6, d), lambda b, s: (b, 0, 0)),
            pl.BlockSpec((1, d), const),
            _resident(w_in, layer),
            pl.BlockSpec((1, GM_INNER), const),
            pl.BlockSpec((1, GM_INNER), const),
            _resident(w_s, layer),
            _resident(bias),
            _resident(w_out, layer),
        ],
        out_specs=pl.BlockSpec((1, tm, d), lambda b, s: (b, s, 0)),
        scratch_shapes=[pltpu.VMEM((tm, GM_INNER), BF16), pltpu.VMEM((tm, GM_INNER), BF16)],
        compiler_params=_params(("parallel", "parallel")),
        name="gmlp_mixer",
    )(x, mod, norm_g.reshape(1, d), w_in, ln_g.reshape(1, -1), ln_b.reshape(1, -1), w_s, bias, w_out)


def _mla_proj_kernel(x_ref, pos_ref, mod_ref, ng_ref, invf_ref, spread_ref, wa_ref, qg_ref, kvg_ref, wq_ref, wqs_ref,
                     wk_ref, wv_ref, q_ref, k_ref, v_ref, cos_scr, sin_scr, *, scale):
    mod = mod_ref[0]
    h = _rms_mod(x_ref[0], ng_ref[...], mod[0:1], mod[1:2]).astype(BF16)
    a = _dot(h, wa_ref[...])
    cq = _rms(a[:, :MLA_Q_RANK], qg_ref[...]).astype(BF16)
    kv_lo = MLA_Q_RANK
    ckv = _rms(a[:, kv_lo:kv_lo + MLA_KV_RANK], kvg_ref[...]).astype(BF16)
    kr_lo = kv_lo + MLA_KV_RANK
    k_rope, k_rope_sw = a[:, kr_lo:kr_lo + LANES], a[:, kr_lo + LANES:kr_lo + 2 * LANES]

    ang = pos_ref[0] * invf_ref[...]
    n = ang.shape[0]

    def spread(table, scr):
        hi = table.astype(BF16)
        rest = table - hi.astype(F32)
        mid = rest.astype(BF16)
        lo = (rest - mid.astype(F32)).astype(BF16)
        y = _dot(jnp.concatenate([hi, mid, lo], axis=0), spread_ref[...])
        z = y[0:n] + y[n:2 * n] + y[2 * n:3 * n]
        for j in range(ROPE_PACK):
            scr[pl.ds(j, n, stride=ROPE_PACK), :] = z[:, j * LANES:(j + 1) * LANES]
        return scr[...]

    cos, sin = spread(jnp.cos(ang), cos_scr), spread(jnp.sin(ang), sin_scr)

    k_r = (k_rope * cos + k_rope_sw * sin).astype(BF16)
    for hp in range(MLA_HEADS // 2):
        q_main = _dot(cq, wq_ref[:, 2 * hp * MLA_HEAD_PAD:2 * (hp + 1) * MLA_HEAD_PAD])
        q_sw = _dot(cq, wqs_ref[:, 2 * hp * LANES:2 * (hp + 1) * LANES])
        k_nope = _dot(ckv, wk_ref[:, 2 * hp * MLA_NOPE:2 * (hp + 1) * MLA_NOPE])
        v_ref[0, :, 2 * hp * MLA_V:2 * (hp + 1) * MLA_V] = _dot(
            ckv, wv_ref[:, 2 * hp * MLA_V:2 * (hp + 1) * MLA_V]).astype(BF16)
        for i in range(2):
            lo = (2 * hp + i) * MLA_HEAD_PAD
            mid = lo + MLA_NOPE
            src = i * MLA_HEAD_PAD
            q_ref[0, :, lo:mid] = (q_main[:, src:src + MLA_NOPE] * scale).astype(BF16)
            q_r = (q_main[:, src + MLA_NOPE:src + MLA_HEAD_PAD] * cos
                   + q_sw[:, i * LANES:(i + 1) * LANES] * sin)
            q_ref[0, :, mid:mid + LANES] = (q_r * scale).astype(BF16)
            k_ref[0, :, lo:mid] = k_nope[:, i * MLA_NOPE:(i + 1) * MLA_NOPE].astype(BF16)
            k_ref[0, :, mid:mid + LANES] = k_r


def _flash_kernel(q_ref, k_ref, v_ref, o_ref, m_scr, l_scr, acc_scr):
    qi = pl.program_id(2)
    t = q_ref.shape[1]

    m_scr[...] = jnp.full_like(m_scr, -jnp.inf)
    l_scr[...] = jnp.zeros_like(l_scr)
    acc_scr[...] = jnp.zeros_like(acc_scr)

    def block(first, n_keys, ends_on_diagonal):
        rows = pl.ds(pl.multiple_of(first * t, t), n_keys)
        for hd in range(FLASH_HEADS):
            qk_lanes = slice(hd * MLA_HEAD_PAD, (hd + 1) * MLA_HEAD_PAD)
            v_lanes = slice(hd * MLA_V, (hd + 1) * MLA_V)
            s = _dot_nt(q_ref[0, :, qk_lanes], k_ref[0, rows, qk_lanes])
            if ends_on_diagonal:
                row = lax.broadcasted_iota(jnp.int32, (t, n_keys), 0)
                col = lax.broadcasted_iota(jnp.int32, (t, n_keys), 1)
                s = jnp.where(row + (n_keys - t) >= col, s, -jnp.inf)
            m_prev = m_scr[hd]
            m_next = jnp.maximum(m_prev, jnp.max(s, axis=1, keepdims=True))
            p = jnp.exp2(s - jnp.concatenate([m_next] * (n_keys // LANES), axis=1))
            alpha = jnp.exp2(m_prev - m_next)
            l_scr[hd] = alpha * l_scr[hd] + jnp.sum(p, axis=1, keepdims=True)
            acc_scr[hd] = alpha * acc_scr[hd] + _dot(p.astype(BF16), v_ref[0, rows, v_lanes])
            m_scr[hd] = m_next

    wide = FLASH_WIDE

    def unmasked(jj, carry):
        block(wide * jj, wide * t, False)
        return carry

    lax.fori_loop(0, qi // wide, unmasked, 0)
    done = (qi // wide) * wide
    for rest in range(wide):
        @pl.when(qi - done == rest)
        def _(rest=rest):
            block(done, (rest + 1) * t, True)
    for hd in range(FLASH_HEADS):
        o_ref[0, :, hd * MLA_V:(hd + 1) * MLA_V] = (acc_scr[hd] / l_scr[hd]).astype(BF16)


def _mla_mixer(x, positions, mod, norm_g, w_a, q_norm_g, kv_norm_g, w_qb, w_kvb, w_o):
    batch, seq, d = x.shape
    heads, rope, half = MLA_HEADS, MLA_ROPE, MLA_ROPE // 2
    pad = LANES - rope

    def swap_halves(w):
        return jnp.concatenate([-w[..., half:], w[..., :half]], axis=-1)

    def pad_lanes(w):
        return jnp.pad(w, [(0, 0)] * (w.ndim - 1) + [(0, pad)])

    kr_lo = MLA_Q_RANK + MLA_KV_RANK
    w_kr = w_a[:, kr_lo:]
    w_a_ext = jnp.concatenate([w_a[:, :kr_lo], pad_lanes(w_kr), pad_lanes(swap_halves(w_kr))], axis=1)

    wq = w_qb.reshape(MLA_Q_RANK, heads, MLA_NOPE + rope)
    wq_rope = wq[:, :, MLA_NOPE:]
    w_q_main = jnp.concatenate([wq[:, :, :MLA_NOPE], pad_lanes(wq_rope)], axis=-1).reshape(MLA_Q_RANK, -1)
    w_q_sw = pad_lanes(swap_halves(wq_rope)).reshape(MLA_Q_RANK, -1)

    wkv = w_kvb.reshape(MLA_KV_RANK, heads, MLA_NOPE + MLA_V)
    w_k = wkv[:, :, :MLA_NOPE].reshape(MLA_KV_RANK, -1)
    w_v = wkv[:, :, MLA_NOPE:].reshape(MLA_KV_RANK, -1)

    inv_freq = ROPE_THETA ** (-jnp.arange(0, rope, 2, dtype=F32) / rope)
    inv_freq = jnp.tile(inv_freq, ROPE_PACK).reshape(1, LANES)
    pos = jnp.repeat(positions.astype(F32).reshape(batch, seq // ROPE_PACK, ROPE_PACK), half, axis=2)
    lane = jnp.arange(LANES)
    spread = jnp.concatenate([lane[:, None] == half * j + lane[None, :] % half for j in range(ROPE_PACK)],
                             axis=1).astype(BF16)

    tm = TM_PROJ
    qk_w = heads * MLA_HEAD_PAD
    const = lambda b, s: (0, 0)
    full = _resident
    weights = [w_a_ext.astype(BF16), q_norm_g.reshape(1, -1), kv_norm_g.reshape(1, -1), w_q_main.astype(BF16),
               w_q_sw.astype(BF16), w_k.astype(BF16), w_v.astype(BF16)]
    q, k, v = pl.pallas_call(
        functools.partial(_mla_proj_kernel, scale=LOG2E * (MLA_NOPE + rope) ** -0.5),
        out_shape=(jax.ShapeDtypeStruct((batch, seq, qk_w), BF16),
                   jax.ShapeDtypeStruct((batch, seq, qk_w), BF16),
                   jax.ShapeDtypeStruct((batch, seq, heads * MLA_V), BF16)),
        grid=(batch, seq // tm),
        in_specs=[
            pl.BlockSpec((1, tm, d), lambda b, s: (b, s, 0)),
            pl.BlockSpec((1, tm // ROPE_PACK, LANES), lambda b, s: (b, s, 0)),
            pl.BlockSpec((1, 6, d), lambda b, s: (b, 0, 0)),
            pl.BlockSpec((1, d), const),
            pl.BlockSpec((1, LANES), const),
            full(spread),
        ] + [full(w) for w in weights],
        out_specs=(pl.BlockSpec((1, tm, qk_w), lambda b, s: (b, s, 0)),
                   pl.BlockSpec((1, tm, qk_w), lambda b, s: (b, s, 0)),
                   pl.BlockSpec((1, tm, heads * MLA_V), lambda b, s: (b, s, 0))),
        scratch_shapes=[pltpu.VMEM((tm, LANES), F32), pltpu.VMEM((tm, LANES), F32)],
        compiler_params=_params(("parallel", "parallel")),
        name="mla_proj",
    )(x, pos, mod, norm_g.reshape(1, d), inv_freq, spread, *weights)

    t = TQ_ATTN
    fh = FLASH_HEADS
    o = pl.pallas_call(
        _flash_kernel,
        out_shape=jax.ShapeDtypeStruct((batch, seq, heads * MLA_V), BF16),
        grid=(batch, heads // fh, seq // t),
        in_specs=[
            pl.BlockSpec((1, t, fh * MLA_HEAD_PAD), lambda b, h, i: (b, i, h)),
            pl.BlockSpec((1, seq, fh * MLA_HEAD_PAD), lambda b, h, i: (b, 0, h)),
            pl.BlockSpec((1, seq, fh * MLA_V), lambda b, h, i: (b, 0, h)),
        ],
        out_specs=pl.BlockSpec((1, t, fh * MLA_V), lambda b, h, i: (b, i, h)),
        scratch_shapes=[pltpu.VMEM((fh, t, LANES), F32), pltpu.VMEM((fh, t, LANES), F32),
                        pltpu.VMEM((fh, t, MLA_V), F32)],
        compiler_params=_params(("parallel", "parallel", "arbitrary")),
        name="mla_flash",
    )(q, k, v)
    return o, w_o.astype(BF16)


def _hgrn_proj_kernel(x_ref, mod_ref, ng_ref, lbp_ref, w_ref, q_ref, k_ref, lf_ref, v_ref, sg_ref, *, layer):
    mod = mod_ref[0]
    h = _rms_mod(x_ref[0], ng_ref[...], mod[0:1], mod[1:2]).astype(BF16)
    nk = HG_HEADS * HG_DK
    nv = HG_HEADS * HG_DV

    lbp = lbp_ref[...]
    e = jnp.exp(lbp - jnp.max(lbp, axis=0, keepdims=True))
    p = e / jnp.sum(e, axis=0, keepdims=True)
    lb = jnp.sum(p[1:layer + 1], axis=0, keepdims=True) if layer >= 1 else jnp.zeros((1, nk), F32)

    def store_heads(ref, val, hd0):
        for i in range(val.shape[1] // HG_DK):
            ref[0, hd0 + i] = val[:, i * HG_DK:(i + 1) * HG_DK]

    pair = 2 * HG_DK
    for hp in range(HG_HEADS // 2):
        cols = lambda seg: slice(seg * nk + hp * pair, seg * nk + (hp + 1) * pair)
        store_heads(q_ref, _dot(h, w_ref[:, cols(0)]).astype(BF16), 2 * hp)
        lb_p = lb[:, hp * pair:(hp + 1) * pair]
        f = lb_p + (1.0 - lb_p) * jax.nn.sigmoid(_dot(h, w_ref[:, cols(1)]))
        store_heads(lf_ref, jnp.log(f), 2 * hp)
        store_heads(k_ref, (1.0 - f).astype(BF16), 2 * hp)
        store_heads(v_ref, _dot(h, w_ref[:, cols(2)]).astype(BF16), 2 * hp)
        store_heads(sg_ref, _silu(_dot(h, w_ref[:, cols(3)])).astype(BF16), 2 * hp)


def _hgrn_chunk(q, kk, g, v, state_t, b_scr):
    c_rows = q.shape[0]
    row = lax.broadcasted_iota(jnp.int32, (c_rows, HG_DK), 0)
    trow = lax.broadcasted_iota(jnp.int32, (c_rows, c_rows), 0)
    tcol = lax.broadcasted_iota(jnp.int32, (c_rows, c_rows), 1)

    b = g
    sh = 1
    while sh < c_rows:
        b = b + jnp.where(row >= sh, pltpu.roll(b, sh, 0), 0.0)
        sh *= 2
    b_scr[...] = b

    attn = jnp.where(trow == tcol, _dot_nt(q.astype(BF16), kk.astype(BF16)), 0.0)
    half = 1
    while half < c_rows:
        right = (row & half) != 0
        if half == 1:
            arg = jnp.where(right, g, 0.0)
        elif half == 2:
            phase = row & 3
            arg = jnp.where(phase == 0, pltpu.roll(g, c_rows - 1, 0),
                            jnp.where(phase == 1, 0.0,
                                      jnp.where(phase == 2, g, g + pltpu.roll(g, 1, 0))))
        else:
            blk = 2 * half
            ref_rows = jnp.concatenate(
                [jnp.broadcast_to(b_scr[m * blk + half - 1:m * blk + half, :], (blk, HG_DK))
                 for m in range(c_rows // blk)], axis=0)
            diff = b - ref_rows
            arg = jnp.where(right, diff, -diff)
        x = jnp.exp(arg)
        q_l = jnp.where(right, q * x, 0.0).astype(BF16)
        k_l = jnp.where(right, 0.0, kk * x).astype(BF16)
        same_block = (trow & -(2 * half)) == (tcol & -(2 * half))
        attn = attn + jnp.where(same_block, _dot_nt(q_l, k_l), 0.0)
        half *= 2

    b_end = b_scr[c_rows - 1:c_rows, :]
    o = _dot(attn.astype(BF16), v)
    o = o + _dot_nt((q * jnp.exp(b)).astype(BF16), state_t.astype(BF16))
    k_dec = (kk * jnp.exp(b_end - b)).astype(BF16)
    new_state_t = jnp.exp(b_end) * state_t + _dot_tn(v, k_dec)
    return o, new_state_t


def _hgrn_rec_kernel(q_ref, k_ref, lf_ref, v_ref, sg_ref, g_ref, o_ref, state_scr, b_scr):
    @pl.when(pl.program_id(1) == 0)
    def _():
        state_scr[...] = jnp.zeros_like(state_scr)

    norm_g = g_ref[...]
    tb = q_ref.shape[2]

    n_chunks = tb // HG_KCHUNK

    def head_pair(hp, carry):
        for i in range(2):
            hd = 2 * hp + i
            state_t = state_scr[hd]
            for c in range(n_chunks):
                rows = pl.ds(c * HG_KCHUNK, HG_KCHUNK)
                o, state_t = _hgrn_chunk(q_ref[0, hd, rows, :].astype(F32), k_ref[0, hd, rows, :].astype(F32),
                                         lf_ref[0, hd, rows, :], v_ref[0, hd, rows, :], state_t,
                                         b_scr.at[i * n_chunks + c])
                o_ref[0, hd, rows, :] = (_rms(o, norm_g) * sg_ref[0, hd, rows, :].astype(F32)).astype(BF16)
            state_scr[hd] = state_t
        return carry

    lax.fori_loop(0, HG_HEADS // 2, head_pair, 0)


def _hgrn_mixer(x, mod, norm_g, hg_lb, layer, w_in, out_norm_g, w_o):
    batch, seq, d = x.shape
    nk = HG_HEADS * HG_DK
    nv = HG_HEADS * HG_DV
    tm = TM_PROJ
    const = lambda b, s: (0, 0)
    heads_shape = lambda w, dt: jax.ShapeDtypeStruct((batch, HG_HEADS, seq, w), dt)
    heads_blk = lambda rows, w: pl.BlockSpec((1, HG_HEADS, rows, w), lambda b, s: (b, 0, s, 0))
    q, k, lf, v, sg = pl.pallas_call(
        functools.partial(_hgrn_proj_kernel, layer=layer),
        out_shape=(heads_shape(HG_DK, BF16), heads_shape(HG_DK, BF16), heads_shape(HG_DK, F32),
                   heads_shape(HG_DV, BF16), heads_shape(HG_DV, BF16)),
        grid=(batch, seq // tm),
        in_specs=[
            pl.BlockSpec((1, tm, d), lambda b, s: (b, s, 0)),
            pl.BlockSpec((1, 6, d), lambda b, s: (b, 0, 0)),
            pl.BlockSpec((1, d), const),
            pl.BlockSpec((DEPTH, nk), const),
            _resident(w_in),
        ],
        out_specs=(heads_blk(tm, HG_DK), heads_blk(tm, HG_DK), heads_blk(tm, HG_DK), heads_blk(tm, HG_DV),
                   heads_blk(tm, HG_DV)),
        compiler_params=_params(("parallel", "parallel")),
        name="hgrn_proj",
    )(x, mod, norm_g.reshape(1, d), hg_lb, w_in)

    tb = TB_HGRN
    o = pl.pallas_call(
        _hgrn_rec_kernel,
        out_shape=heads_shape(HG_DV, BF16),
        grid=(batch, seq // tb),
        in_specs=[heads_blk(tb, HG_DK), heads_blk(tb, HG_DK), heads_blk(tb, HG_DK), heads_blk(tb, HG_DV),
                  heads_blk(tb, HG_DV), pl.BlockSpec((1, HG_DV), const)],
        out_specs=heads_blk(tb, HG_DV),
        scratch_shapes=[pltpu.VMEM((HG_HEADS, HG_DV, HG_DK), F32),
                        pltpu.VMEM((2 * tb // HG_KCHUNK, HG_KCHUNK, HG_DK), F32)],
        compiler_params=_params(("parallel", "arbitrary")),
        name="hgrn_recurrence",
    )(q, k, lf, v, sg, out_norm_g.reshape(1, HG_DV))
    return o, w_o


def kernel(x, c, positions, ada_w, ada_b, mix_norm_g, ffn_norm_g, gm_w_in, gm_ln_g, gm_ln_b, gm_w_s, gm_b_s,
           gm_w_out, mla_w_a, mla_q_norm_g, mla_kv_norm_g, mla_w_qb, mla_w_kvb, mla_w_o, hg_lb, hg_w_in,
           hg_norm_g, hg_w_o, ff_w_up, ff_conv_w, ff_conv_b, ff_w_down, final_g):
    mod_all = _adaln(c, ada_w, ada_b)
    gm_w_in, gm_w_out = gm_w_in.astype(BF16), gm_w_out.astype(BF16)
    ff_w_up, ff_w_down = ff_w_up.astype(BF16), ff_w_down.astype(BF16)
    for i in range(DEPTH):
        mod = mod_all[i]
        kind, j = i % N_MIXERS, i // N_MIXERS
        proj = None
        if kind == 0:
            x = _gmlp_mixer(x, mod, mix_norm_g[i], j, gm_w_in, gm_ln_g[j], gm_ln_b[j], gm_w_s, gm_b_s[j], gm_w_out)
        elif kind == 1:
            proj = _mla_mixer(x, positions, mod, mix_norm_g[i], mla_w_a[j], mla_q_norm_g[j], mla_kv_norm_g[j],
                              mla_w_qb[j], mla_w_kvb[j], mla_w_o[j])
        else:
            proj = _hgrn_mixer(x, mod, mix_norm_g[i], hg_lb, i, hg_w_in[j].astype(BF16), hg_norm_g[j],
                               hg_w_o[j].astype(BF16))
        x = _conv_ffn(x, mod, ffn_norm_g[i], i, ff_w_up, ff_conv_w, ff_conv_b, ff_w_down, final_g,
                      final_norm=(i == DEPTH - 1), proj=proj)
    return x
```

```python
import functools

import jax
import jax.numpy as jnp
from jax import lax
from jax.experimental import pallas as pl
from jax.experimental.pallas import tpu as pltpu

F32 = jnp.float32
BF16 = jnp.bfloat16

D_MODEL = 1024
DEPTH = 4
N_MIXERS = 3
EPS = 1e-6

GM_CHUNK = 128
GM_GROUPS = 8
GM_INNER = 2 * D_MODEL
GM_GROUP_DIM = GM_INNER // GM_GROUPS

MLA_HEADS = 8
MLA_Q_RANK = D_MODEL // 4
MLA_KV_RANK = D_MODEL // 8
MLA_NOPE = 128
MLA_ROPE = 64
MLA_V = 128
ROPE_THETA = 10000.0

HG_HEADS = 8
HG_DK = 128
HG_DV = D_MODEL // HG_HEADS

FF_DIM = 2816
CONV_W = 3

LOG2E = 1.4426950408889634
LANES = 128
CONV_TAIL = 8
VMEM_LIMIT = 56 * 1024 * 1024
ROPE_PACK = LANES // (MLA_ROPE // 2)
MLA_HEAD_PAD = MLA_NOPE + LANES

TM_FFN = 1024
FC_FFN = 256
TM_GMLP = 1024
TM_PROJ = 1024
TQ_ATTN = 512
FLASH_WIDE = 4
FLASH_HEADS = 4
TB_HGRN = 1024
HG_KCHUNK = 128
HG_TRIP_HEADS = 4


def _params(sem):
    return pltpu.CompilerParams(dimension_semantics=sem, vmem_limit_bytes=VMEM_LIMIT)


def _resident(arr, layer=None):
    if layer is None:
        shape, index = arr.shape, (0,) * arr.ndim
    else:
        shape, index = (None,) + arr.shape[1:], (layer,) + (0,) * (arr.ndim - 1)
    return pl.BlockSpec(shape, lambda *_: index, pipeline_mode=pl.Buffered(1))


def _rms(x, g):
    return x * lax.rsqrt(jnp.mean(x * x, axis=-1, keepdims=True) + EPS) * g


def _rms_mod(x, g, shift, scale):
    return _rms(x, g * (1.0 + scale)) + shift


def _dot(a, b):
    return jnp.dot(a, b, preferred_element_type=F32)


def _dot_nt(a, b):
    return lax.dot_general(a, b, (((1,), (1,)), ((), ())), preferred_element_type=F32)


def _dot_tn(a, b):
    return lax.dot_general(a, b, (((0,), (0,)), ((), ())), preferred_element_type=F32)


def _silu(x):
    return x / (1.0 + jnp.exp2(x * (-LOG2E)))


def _gelu_tanh(x):
    c1 = -2.0 * LOG2E * 0.7978845608028654
    return x / (1.0 + jnp.exp2(x * (c1 + (c1 * 0.044715) * (x * x))))


def _adaln_kernel(c_ref, w_ref, b_ref, o_ref):
    c_act = _silu(c_ref[...]).astype(BF16)
    o_ref[0] = _dot(c_act, w_ref[0].astype(BF16)) + b_ref[0]


def _adaln(c, ada_w, ada_b):
    batch = c.shape[0]
    rows = 8
    tn = 1536
    c_pad = jnp.pad(c, ((0, rows - batch), (0, 0)))
    n = 6 * D_MODEL
    mod = pl.pallas_call(
        _adaln_kernel,
        out_shape=jax.ShapeDtypeStruct((DEPTH, rows, n), F32),
        grid=(DEPTH, n // tn),
        in_specs=[
            pl.BlockSpec((rows, D_MODEL), lambda i, j: (0, 0)),
            pl.BlockSpec((1, D_MODEL, tn), lambda i, j: (i, 0, j)),
            pl.BlockSpec((1, 1, tn), lambda i, j: (i, 0, j)),
        ],
        out_specs=pl.BlockSpec((1, rows, tn), lambda i, j: (i, 0, j)),
        compiler_params=_params(("parallel", "parallel")),
        name="adaln_mod",
    )(c_pad, ada_w, ada_b.reshape(DEPTH, 1, n))
    return mod[:, :batch].reshape(DEPTH, batch, 6, D_MODEL)


def _ffn_kernel(*refs, final_norm, fused_proj):
    if fused_proj:
        a_ref, wo_ref, *refs = refs
    x_ref, mod_ref, ng_ref, wu_ref, cw_ref, cb_ref, wd_ref, fg_ref, o_ref, h_scr, act_scr, tail_scr, x_scr = refs
    mod = mod_ref[0]
    tm = x_ref.shape[1]
    tail = tail_scr.shape[1]
    n_chunks = FF_DIM // FC_FFN

    x = x_ref[0]
    if fused_proj:
        if len(a_ref.shape) == 4:
            a = jnp.concatenate([a_ref[0, hd] for hd in range(a_ref.shape[1])], axis=-1)
        else:
            a = a_ref[0]
        x = x + (1.0 + mod[2:3]) * _dot(a, wo_ref[...])
        x_scr[...] = x
    h_scr[...] = _rms_mod(x, ng_ref[...], mod[3:4], mod[4:5]).astype(BF16)
    h = h_scr[...]

    @pl.when(pl.program_id(1) == 0)
    def _():
        tail_scr[...] = jnp.zeros_like(tail_scr)

    def conv(ci):
        lo = ci * FC_FFN
        a = _dot(h, wu_ref[:, lo:lo + FC_FFN])
        ext = jnp.concatenate([tail_scr[ci], a], axis=0)
        tail_scr[ci] = a[tm - tail:]
        cw = cw_ref[:, lo:lo + FC_FFN]
        return (cb_ref[:, lo:lo + FC_FFN] + cw[2:3] * a + cw[1:2] * pltpu.roll(ext, 1, 0)[tail:]
                + cw[0:1] * pltpu.roll(ext, 2, 0)[tail:])

    for c in range(n_chunks):
        act_scr[:, c * FC_FFN:(c + 1) * FC_FFN] = (_silu(conv(c)) * conv(n_chunks + c)).astype(BF16)

    x = x_scr[...] if fused_proj else x_ref[0]
    out = x + (1.0 + mod[5:6]) * _dot(act_scr[...], wd_ref[...])
    if final_norm:
        out = _rms(out, fg_ref[...])
    o_ref[0] = out


def _conv_ffn(x, mod, norm_g, layer, w_up, conv_w, conv_b, w_down, final_g, final_norm, proj=None):
    batch, seq, d = x.shape
    tm = TM_FFN
    const = lambda b, s: (0, 0)
    conv_b = conv_b.reshape(DEPTH, 1, -1)
    tok = pl.BlockSpec((1, tm, d), lambda b, s: (b, s, 0))
    operands, in_specs = [], []
    if proj is not None:
        a, w_o = proj
        if a.ndim == 4:
            in_specs.append(pl.BlockSpec((1, a.shape[1], tm, a.shape[3]), lambda b, s: (b, 0, s, 0)))
        else:
            in_specs.append(pl.BlockSpec((1, tm, a.shape[2]), lambda b, s: (b, s, 0)))
        in_specs.append(_resident(w_o))
        operands += [a, w_o]
    operands += [x, mod, norm_g.reshape(1, d), w_up, conv_w, conv_b, w_down, final_g.reshape(1, d)]
    in_specs += [tok, pl.BlockSpec((1, 6, d), lambda b, s: (b, 0, 0)), pl.BlockSpec((1, d), const),
                 _resident(w_up, layer), _resident(conv_w, layer), _resident(conv_b, layer),
                 _resident(w_down, layer), pl.BlockSpec((1, d), const)]
    return pl.pallas_call(
        functools.partial(_ffn_kernel, final_norm=final_norm, fused_proj=proj is not None),
        out_shape=jax.ShapeDtypeStruct(x.shape, F32),
        grid=(batch, seq // tm),
        in_specs=in_specs,
        out_specs=tok,
        scratch_shapes=[pltpu.VMEM((tm, d), BF16), pltpu.VMEM((tm, FF_DIM), BF16),
                        pltpu.VMEM((2 * FF_DIM // FC_FFN, CONV_TAIL, FC_FFN), F32),
                        pltpu.VMEM((tm, d) if proj is not None else (8, LANES), F32)],
        compiler_params=_params(("parallel", "arbitrary")),
        name="conv_ffn",
    )(*operands)


def _gmlp_kernel(x_ref, mod_ref, ng_ref, win_ref, lng_ref, lnb_ref, ws_ref, bs_ref, wout_ref, o_ref, v_scr,
                 y_scr):
    mod = mod_ref[0]
    x = x_ref[0]
    tm = x.shape[0]
    h = _rms_mod(x, ng_ref[...], mod[0:1], mod[1:2]).astype(BF16)

    v = _gelu_tanh(_dot(h, win_ref[:, GM_INNER:]))
    mu = jnp.mean(v, axis=-1, keepdims=True)
    vc = v - mu
    var = jnp.mean(vc * vc, axis=-1, keepdims=True)
    v_scr[...] = (vc * lax.rsqrt(var + EPS) * lng_ref[...] + lnb_ref[...]).astype(BF16)

    row = lax.broadcasted_iota(jnp.int32, (GM_CHUNK, GM_CHUNK), 0)
    col = lax.broadcasted_iota(jnp.int32, (GM_CHUNK, GM_CHUNK), 1)
    causal = row >= col

    for g in range(GM_GROUPS):
        lo, hi = g * GM_GROUP_DIM, (g + 1) * GM_GROUP_DIM
        u = _gelu_tanh(_dot(h, win_ref[:, lo:hi]))
        w_s = jnp.where(causal, ws_ref[g], 0.0).astype(BF16)
        bias = bs_ref[:, lo:hi]
        for c in range(tm // GM_CHUNK):
            rows = slice(c * GM_CHUNK, (c + 1) * GM_CHUNK)
            sv = _dot(w_s, v_scr[rows, lo:hi]) + bias
            y_scr[rows, lo:hi] = (u[rows] * sv).astype(BF16)
    o_ref[0] = x + (1.0 + mod[2:3]) * _dot(y_scr[...], wout_ref[...])


def _gmlp_mixer(x, mod, norm_g, layer, w_in, ln_g, ln_b, w_s, b_s, w_out):
    batch, seq, d = x.shape
    tm = TM_GMLP
    bias = jnp.repeat(b_s.T, GM_GROUP_DIM, axis=1)
    const = lambda b, s: (0, 0)
    return pl.pallas_call(
        _gmlp_kernel,
        out_shape=jax.ShapeDtypeStruct(x.shape, F32),
        grid=(batch, seq // tm),
        in_specs=[
            pl.BlockSpec((1, tm, d), lambda b, s: (b, s, 0)),
            pl.BlockSpec((1, 6, d), lambda b, s: (b, 0, 0)),
            pl.BlockSpec((1, d), const),
            _resident(w_in, layer),
            pl.BlockSpec((1, GM_INNER), const),
            pl.BlockSpec((1, GM_INNER), const),
            _resident(w_s, layer),
            _resident(bias),
            _resident(w_out, layer),
        ],
        out_specs=pl.BlockSpec((1, tm, d), lambda b, s: (b, s, 0)),
        scratch_shapes=[pltpu.VMEM((tm, GM_INNER), BF16), pltpu.VMEM((tm, GM_INNER), BF16)],
        compiler_params=_params(("parallel", "parallel")),
        name="gmlp_mixer",
    )(x, mod, norm_g.reshape(1, d), w_in, ln_g.reshape(1, -1), ln_b.reshape(1, -1), w_s, bias, w_out)


def _mla_proj_kernel(x_ref, pos_ref, mod_ref, ng_ref, invf_ref, spread_ref, wa_ref, qg_ref, kvg_ref, wq_ref, wqs_ref,
                     wk_ref, wv_ref, q_ref, k_ref, v_ref, cos_scr, sin_scr, *, scale):
    mod = mod_ref[0]
    h = _rms_mod(x_ref[0], ng_ref[...], mod[0:1], mod[1:2]).astype(BF16)
    a = _dot(h, wa_ref[...])
    cq = _rms(a[:, :MLA_Q_RANK], qg_ref[...]).astype(BF16)
    kv_lo = MLA_Q_RANK
    ckv = _rms(a[:, kv_lo:kv_lo + MLA_KV_RANK], kvg_ref[...]).astype(BF16)
    kr_lo = kv_lo + MLA_KV_RANK
    k_rope, k_rope_sw = a[:, kr_lo:kr_lo + LANES], a[:, kr_lo + LANES:kr_lo + 2 * LANES]

    ang = pos_ref[0] * invf_ref[...]
    n = ang.shape[0]

    def spread(table, scr):
        hi = table.astype(BF16)
        rest = table - hi.astype(F32)
        mid = rest.astype(BF16)
        lo = (rest - mid.astype(F32)).astype(BF16)
        y = _dot(jnp.concatenate([hi, mid, lo], axis=0), spread_ref[...])
        z = y[0:n] + y[n:2 * n] + y[2 * n:3 * n]
        for j in range(ROPE_PACK):
            scr[pl.ds(j, n, stride=ROPE_PACK), :] = z[:, j * LANES:(j + 1) * LANES]
        return scr[...]

    cos, sin = spread(jnp.cos(ang), cos_scr), spread(jnp.sin(ang), sin_scr)

    k_r = (k_rope * cos + k_rope_sw * sin).astype(BF16)
    for hp in range(MLA_HEADS // 2):
        q_main = _dot(cq, wq_ref[:, 2 * hp * MLA_HEAD_PAD:2 * (hp + 1) * MLA_HEAD_PAD])
        q_sw = _dot(cq, wqs_ref[:, 2 * hp * LANES:2 * (hp + 1) * LANES])
        k_nope = _dot(ckv, wk_ref[:, 2 * hp * MLA_NOPE:2 * (hp + 1) * MLA_NOPE])
        v_ref[0, :, 2 * hp * MLA_V:2 * (hp + 1) * MLA_V] = _dot(
            ckv, wv_ref[:, 2 * hp * MLA_V:2 * (hp + 1) * MLA_V]).astype(BF16)
        for i in range(2):
            lo = (2 * hp + i) * MLA_HEAD_PAD
            mid = lo + MLA_NOPE
            src = i * MLA_HEAD_PAD
            q_ref[0, :, lo:mid] = (q_main[:, src:src + MLA_NOPE] * scale).astype(BF16)
            q_r = (q_main[:, src + MLA_NOPE:src + MLA_HEAD_PAD] * cos
                   + q_sw[:, i * LANES:(i + 1) * LANES] * sin)
            q_ref[0, :, mid:mid + LANES] = (q_r * scale).astype(BF16)
            k_ref[0, :, lo:mid] = k_nope[:, i * MLA_NOPE:(i + 1) * MLA_NOPE].astype(BF16)
            k_ref[0, :, mid:mid + LANES] = k_r


def _flash_kernel(q_ref, k_ref, v_ref, o_ref, m_scr, l_scr, acc_scr):
    qi = pl.program_id(2)
    t = q_ref.shape[1]

    m_scr[...] = jnp.full_like(m_scr, -jnp.inf)
    l_scr[...] = jnp.zeros_like(l_scr)
    acc_scr[...] = jnp.zeros_like(acc_scr)

    def block(first, n_keys, ends_on_diagonal):
        rows = pl.ds(pl.multiple_of(first * t, t), n_keys)
        for hd in range(FLASH_HEADS):
            qk_lanes = slice(hd * MLA_HEAD_PAD, (hd + 1) * MLA_HEAD_PAD)
            v_lanes = slice(hd * MLA_V, (hd + 1) * MLA_V)
            s = _dot_nt(q_ref[0, :, qk_lanes], k_ref[0, rows, qk_lanes])
            if ends_on_diagonal:
                row = lax.broadcasted_iota(jnp.int32, (t, n_keys), 0)
                col = lax.broadcasted_iota(jnp.int32, (t, n_keys), 1)
                s = jnp.where(row + (n_keys - t) >= col, s, -jnp.inf)
            m_prev = m_scr[hd]
            m_next = jnp.maximum(m_prev, jnp.max(s, axis=1, keepdims=True))
            p = jnp.exp2(s - jnp.concatenate([m_next] * (n_keys // LANES), axis=1))
            alpha = jnp.exp2(m_prev - m_next)
            l_scr[hd] = alpha * l_scr[hd] + jnp.sum(p, axis=1, keepdims=True)
            acc_scr[hd] = alpha * acc_scr[hd] + _dot(p.astype(BF16), v_ref[0, rows, v_lanes])
            m_scr[hd] = m_next

    wide = FLASH_WIDE

    def unmasked(jj, carry):
        block(wide * jj, wide * t, False)
        return carry

    lax.fori_loop(0, qi // wide, unmasked, 0)
    done = (qi // wide) * wide
    for rest in range(wide):
        @pl.when(qi - done == rest)
        def _(rest=rest):
            block(done, (rest + 1) * t, True)
    for hd in range(FLASH_HEADS):
        o_ref[0, :, hd * MLA_V:(hd + 1) * MLA_V] = (acc_scr[hd] / l_scr[hd]).astype(BF16)


def _mla_mixer(x, positions, mod, norm_g, w_a, q_norm_g, kv_norm_g, w_qb, w_kvb, w_o):
    batch, seq, d = x.shape
    heads, rope, half = MLA_HEADS, MLA_ROPE, MLA_ROPE // 2
    pad = LANES - rope

    def swap_halves(w):
        return jnp.concatenate([-w[..., half:], w[..., :half]], axis=-1)

    def pad_lanes(w):
        return jnp.pad(w, [(0, 0)] * (w.ndim - 1) + [(0, pad)])

    kr_lo = MLA_Q_RANK + MLA_KV_RANK
    w_kr = w_a[:, kr_lo:]
    w_a_ext = jnp.concatenate([w_a[:, :kr_lo], pad_lanes(w_kr), pad_lanes(swap_halves(w_kr))], axis=1)

    wq = w_qb.reshape(MLA_Q_RANK, heads, MLA_NOPE + rope)
    wq_rope = wq[:, :, MLA_NOPE:]
    w_q_main = jnp.concatenate([wq[:, :, :MLA_NOPE], pad_lanes(wq_rope)], axis=-1).reshape(MLA_Q_RANK, -1)
    w_q_sw = pad_lanes(swap_halves(wq_rope)).reshape(MLA_Q_RANK, -1)

    wkv = w_kvb.reshape(MLA_KV_RANK, heads, MLA_NOPE + MLA_V)
    w_k = wkv[:, :, :MLA_NOPE].reshape(MLA_KV_RANK, -1)
    w_v = wkv[:, :, MLA_NOPE:].reshape(MLA_KV_RANK, -1)

    inv_freq = ROPE_THETA ** (-jnp.arange(0, rope, 2, dtype=F32) / rope)
    inv_freq = jnp.tile(inv_freq, ROPE_PACK).reshape(1, LANES)
    pos = jnp.repeat(positions.astype(F32).reshape(batch, seq // ROPE_PACK, ROPE_PACK), half, axis=2)
    lane = jnp.arange(LANES)
    spread = jnp.concatenate([lane[:, None] == half * j + lane[None, :] % half for j in range(ROPE_PACK)],
                             axis=1).astype(BF16)

    tm = TM_PROJ
    qk_w = heads * MLA_HEAD_PAD
    const = lambda b, s: (0, 0)
    full = _resident
    weights = [w_a_ext.astype(BF16), q_norm_g.reshape(1, -1), kv_norm_g.reshape(1, -1), w_q_main.astype(BF16),
               w_q_sw.astype(BF16), w_k.astype(BF16), w_v.astype(BF16)]
    q, k, v = pl.pallas_call(
        functools.partial(_mla_proj_kernel, scale=LOG2E * (MLA_NOPE + rope) ** -0.5),
        out_shape=(jax.ShapeDtypeStruct((batch, seq, qk_w), BF16),
                   jax.ShapeDtypeStruct((batch, seq, qk_w), BF16),
                   jax.ShapeDtypeStruct((batch, seq, heads * MLA_V), BF16)),
        grid=(batch, seq // tm),
        in_specs=[
            pl.BlockSpec((1, tm, d), lambda b, s: (b, s, 0)),
            pl.BlockSpec((1, tm // ROPE_PACK, LANES), lambda b, s: (b, s, 0)),
            pl.BlockSpec((1, 6, d), lambda b, s: (b, 0, 0)),
            pl.BlockSpec((1, d), const),
            pl.BlockSpec((1, LANES), const),
            full(spread),
        ] + [full(w) for w in weights],
        out_specs=(pl.BlockSpec((1, tm, qk_w), lambda b, s: (b, s, 0)),
                   pl.BlockSpec((1, tm, qk_w), lambda b, s: (b, s, 0)),
                   pl.BlockSpec((1, tm, heads * MLA_V), lambda b, s: (b, s, 0))),
        scratch_shapes=[pltpu.VMEM((tm, LANES), F32), pltpu.VMEM((tm, LANES), F32)],
        compiler_params=_params(("parallel", "parallel")),
        name="mla_proj",
    )(x, pos, mod, norm_g.reshape(1, d), inv_freq, spread, *weights)

    t = TQ_ATTN
    fh = FLASH_HEADS
    o = pl.pallas_call(
        _flash_kernel,
        out_shape=jax.ShapeDtypeStruct((batch, seq, heads * MLA_V), BF16),
        grid=(batch, heads // fh, seq // t),
        in_specs=[
            pl.BlockSpec((1, t, fh * MLA_HEAD_PAD), lambda b, h, i: (b, i, h)),
            pl.BlockSpec((1, seq, fh * MLA_HEAD_PAD), lambda b, h, i: (b, 0, h)),
            pl.BlockSpec((1, seq, fh * MLA_V), lambda b, h, i: (b, 0, h)),
        ],
        out_specs=pl.BlockSpec((1, t, fh * MLA_V), lambda b, h, i: (b, i, h)),
        scratch_shapes=[pltpu.VMEM((fh, t, LANES), F32), pltpu.VMEM((fh, t, LANES), F32),
                        pltpu.VMEM((fh, t, MLA_V), F32)],
        compiler_params=_params(("parallel", "parallel", "arbitrary")),
        name="mla_flash",
    )(q, k, v)
    return o, w_o.astype(BF16)


def _hgrn_proj_kernel(x_ref, mod_ref, ng_ref, lbp_ref, w_ref, q_ref, k_ref, lf_ref, v_ref, sg_ref, *, layer):
    mod = mod_ref[0]
    h = _rms_mod(x_ref[0], ng_ref[...], mod[0:1], mod[1:2]).astype(BF16)
    nk = HG_HEADS * HG_DK
    nv = HG_HEADS * HG_DV

    lbp = lbp_ref[...]
    e = jnp.exp(lbp - jnp.max(lbp, axis=0, keepdims=True))
    p = e / jnp.sum(e, axis=0, keepdims=True)
    lb = jnp.sum(p[1:layer + 1], axis=0, keepdims=True) if layer >= 1 else jnp.zeros((1, nk), F32)

    def store_heads(ref, val, hd0):
        for i in range(val.shape[1] // HG_DK):
            ref[0, hd0 + i] = val[:, i * HG_DK:(i + 1) * HG_DK]

    pair = 2 * HG_DK
    for hp in range(HG_HEADS // 2):
        cols = lambda seg: slice(seg * nk + hp * pair, seg * nk + (hp + 1) * pair)
        store_heads(q_ref, _dot(h, w_ref[:, cols(0)]).astype(BF16), 2 * hp)
        lb_p = lb[:, hp * pair:(hp + 1) * pair]
        f = lb_p + (1.0 - lb_p) * jax.nn.sigmoid(_dot(h, w_ref[:, cols(1)]))
        store_heads(lf_ref, jnp.log(f), 2 * hp)
        store_heads(k_ref, (1.0 - f).astype(BF16), 2 * hp)
        store_heads(v_ref, _dot(h, w_ref[:, cols(2)]).astype(BF16), 2 * hp)
        store_heads(sg_ref, _silu(_dot(h, w_ref[:, cols(3)])).astype(BF16), 2 * hp)


def _hgrn_chunk(q, kk, g, v, state_t, b_scr):
    c_rows = q.shape[0]
    row = lax.broadcasted_iota(jnp.int32, (c_rows, HG_DK), 0)
    trow = lax.broadcasted_iota(jnp.int32, (c_rows, c_rows), 0)
    tcol = lax.broadcasted_iota(jnp.int32, (c_rows, c_rows), 1)

    b = g
    sh = 1
    while sh < c_rows:
        b = b + jnp.where(row >= sh, pltpu.roll(b, sh, 0), 0.0)
        sh *= 2
    b_scr[...] = b

    attn = jnp.where(trow == tcol, _dot_nt(q.astype(BF16), kk.astype(BF16)), 0.0)
    half = 1
    while half < c_rows:
        right = (row & half) != 0
        if half == 1:
            arg = jnp.where(right, g, 0.0)
        elif half == 2:
            phase = row & 3
            arg = jnp.where(phase == 0, pltpu.roll(g, c_rows - 1, 0),
                            jnp.where(phase == 1, 0.0,
                                      jnp.where(phase == 2, g, g + pltpu.roll(g, 1, 0))))
        else:
            blk = 2 * half
            ref_rows = jnp.concatenate(
                [jnp.broadcast_to(b_scr[m * blk + half - 1:m * blk + half, :], (blk, HG_DK))
                 for m in range(c_rows // blk)], axis=0)
            diff = b - ref_rows
            arg = jnp.where(right, diff, -diff)
        x = jnp.exp(arg)
        q_l = jnp.where(right, q * x, 0.0).astype(BF16)
        k_l = jnp.where(right, 0.0, kk * x).astype(BF16)
        same_block = (trow & -(2 * half)) == (tcol & -(2 * half))
        attn = attn + jnp.where(same_block, _dot_nt(q_l, k_l), 0.0)
        half *= 2

    b_end = b_scr[c_rows - 1:c_rows, :]
    o = _dot(attn.astype(BF16), v)
    o = o + _dot_nt((q * jnp.exp(b)).astype(BF16), state_t.astype(BF16))
    k_dec = (kk * jnp.exp(b_end - b)).astype(BF16)
    new_state_t = jnp.exp(b_end) * state_t + _dot_tn(v, k_dec)
    return o, new_state_t


def _hgrn_rec_kernel(q_ref, k_ref, lf_ref, v_ref, sg_ref, g_ref, o_ref, state_scr, b_scr):
    @pl.when(pl.program_id(1) == 0)
    def _():
        state_scr[...] = jnp.zeros_like(state_scr)

    norm_g = g_ref[...]
    tb = q_ref.shape[2]

    n_chunks = tb // HG_KCHUNK

    def head_group(hg, carry):
        for i in range(HG_TRIP_HEADS):
            hd = HG_TRIP_HEADS * hg + i
            state_t = state_scr[hd]
            for c in range(n_chunks):
                rows = pl.ds(c * HG_KCHUNK, HG_KCHUNK)
                o, state_t = _hgrn_chunk(q_ref[0, hd, rows, :].astype(F32), k_ref[0, hd, rows, :].astype(F32),
                                         lf_ref[0, hd, rows, :], v_ref[0, hd, rows, :], state_t,
                                         b_scr.at[i * n_chunks + c])
                o_ref[0, hd, rows, :] = (_rms(o, norm_g) * sg_ref[0, hd, rows, :].astype(F32)).astype(BF16)
            state_scr[hd] = state_t
        return carry

    lax.fori_loop(0, HG_HEADS // HG_TRIP_HEADS, head_group, 0)


def _hgrn_mixer(x, mod, norm_g, hg_lb, layer, w_in, out_norm_g, w_o):
    batch, seq, d = x.shape
    nk = HG_HEADS * HG_DK
    nv = HG_HEADS * HG_DV
    tm = TM_PROJ
    const = lambda b, s: (0, 0)
    heads_shape = lambda w, dt: jax.ShapeDtypeStruct((batch, HG_HEADS, seq, w), dt)
    heads_blk = lambda rows, w: pl.BlockSpec((1, HG_HEADS, rows, w), lambda b, s: (b, 0, s, 0))
    q, k, lf, v, sg = pl.pallas_call(
        functools.partial(_hgrn_proj_kernel, layer=layer),
        out_shape=(heads_shape(HG_DK, BF16), heads_shape(HG_DK, BF16), heads_shape(HG_DK, F32),
                   heads_shape(HG_DV, BF16), heads_shape(HG_DV, BF16)),
        grid=(batch, seq // tm),
        in_specs=[
            pl.BlockSpec((1, tm, d), lambda b, s: (b, s, 0)),
            pl.BlockSpec((1, 6, d), lambda b, s: (b, 0, 0)),
            pl.BlockSpec((1, d), const),
            pl.BlockSpec((DEPTH, nk), const),
            _resident(w_in),
        ],
        out_specs=(heads_blk(tm, HG_DK), heads_blk(tm, HG_DK), heads_blk(tm, HG_DK), heads_blk(tm, HG_DV),
                   heads_blk(tm, HG_DV)),
        compiler_params=_params(("parallel", "parallel")),
        name="hgrn_proj",
    )(x, mod, norm_g.reshape(1, d), hg_lb, w_in)

    tb = TB_HGRN
    o = pl.pallas_call(
        _hgrn_rec_kernel,
        out_shape=heads_shape(HG_DV, BF16),
        grid=(batch, seq // tb),
        in_specs=[heads_blk(tb, HG_DK), heads_blk(tb, HG_DK), heads_blk(tb, HG_DK), heads_blk(tb, HG_DV),
                  heads_blk(tb, HG_DV), pl.BlockSpec((1, HG_DV), const)],
        out_specs=heads_blk(tb, HG_DV),
        scratch_shapes=[pltpu.VMEM((HG_HEADS, HG_DV, HG_DK), F32),
                        pltpu.VMEM((HG_TRIP_HEADS * tb // HG_KCHUNK, HG_KCHUNK, HG_DK), F32)],
        compiler_params=_params(("parallel", "arbitrary")),
        name="hgrn_recurrence",
    )(q, k, lf, v, sg, out_norm_g.reshape(1, HG_DV))
    return o, w_o


def kernel(x, c, positions, ada_w, ada_b, mix_norm_g, ffn_norm_g, gm_w_in, gm_ln_g, gm_ln_b, gm_w_s, gm_b_s,
           gm_w_out, mla_w_a, mla_q_norm_g, mla_kv_norm_g, mla_w_qb, mla_w_kvb, mla_w_o, hg_lb, hg_w_in,
           hg_norm_g, hg_w_o, ff_w_up, ff_conv_w, ff_conv_b, ff_w_down, final_g):
    mod_all = _adaln(c, ada_w, ada_b)
    gm_w_in, gm_w_out = gm_w_in.astype(BF16), gm_w_out.astype(BF16)
    ff_w_up, ff_w_down = ff_w_up.astype(BF16), ff_w_down.astype(BF16)
    for i in range(DEPTH):
        mod = mod_all[i]
        kind, j = i % N_MIXERS, i // N_MIXERS
        proj = None
        if kind == 0:
            x = _gmlp_mixer(x, mod, mix_norm_g[i], j, gm_w_in, gm_ln_g[j], gm_ln_b[j], gm_w_s, gm_b_s[j], gm_w_out)
        elif kind == 1:
            proj = _mla_mixer(x, positions, mod, mix_norm_g[i], mla_w_a[j], mla_q_norm_g[j], mla_kv_norm_g[j],
                              mla_w_qb[j], mla_w_kvb[j], mla_w_o[j])
        else:
            proj = _hgrn_mixer(x, mod, mix_norm_g[i], hg_lb, i, hg_w_in[j].astype(BF16), hg_norm_g[j],
                               hg_w_o[j].astype(BF16))
        x = _conv_ffn(x, mod, ffn_norm_g[i], i, ff_w_up, ff_conv_w, ff_conv_b, ff_w_down, final_g,
                      final_norm=(i == DEPTH - 1), proj=proj)
    return x
```
